```python
import jax, jax.numpy as jnp
from jax import lax
import numpy as np

D_MODEL = 1024
BATCH = 16
SEQ = 2048
DEPTH = 4

HEAD_DIM = 64
N_Q_HEADS = 8
N_KV_HEADS = 2
Q_PER_KV = N_Q_HEADS // N_KV_HEADS
ATTN_WIDTH = N_Q_HEADS * HEAD_DIM
KV_WIDTH = N_KV_HEADS * HEAD_DIM
CONV_WIDTH = D_MODEL - ATTN_WIDTH
MIX_WIDTH = ATTN_WIDTH + CONV_WIDTH
N_MIX_HEADS = MIX_WIDTH // HEAD_DIM
Q_END = ATTN_WIDTH
K_END = Q_END + KV_WIDTH
V_END = K_END + KV_WIDTH
CB_END = V_END + CONV_WIDTH
CC_END = CB_END + CONV_WIDTH
IN_WIDTH = CC_END + CONV_WIDTH
WINDOW = 128
ATTN_BLOCK = 128
ROPE_THETA = 10000.0
CONV_K = 3
D_FF = 3584
N_EXPERTS = 8
TOP_K = 2
MOE_BLOCK = 128
N_DENSE = (DEPTH + 1) // 2
N_MOE = DEPTH // 2
NORM_EPS = 1e-5
NEG_INF = -1e30
MAX_POS_OFFSET = 4096

kernel_name = 'hybrid_swa_sink_shortconv_moe_trunk'


def rms_norm(x, w):
    xf = x.astype(jnp.float32)
    var = jnp.mean(xf * xf, axis=-1, keepdims=True)
    return (xf * lax.rsqrt(var + NORM_EPS) * w.astype(jnp.float32)).astype(x.dtype)


def rope_tables(positions):
    inv_freq = ROPE_THETA ** (-jnp.arange(0, HEAD_DIM, 2, dtype=jnp.float32) / HEAD_DIM)
    ang = positions.astype(jnp.float32)[..., None] * inv_freq
    return jnp.cos(ang)[:, :, None, :], jnp.sin(ang)[:, :, None, :]


def apply_rope(t, cos, sin):
    half = HEAD_DIM // 2
    tf = t.astype(jnp.float32)
    t1, t2 = tf[..., :half], tf[..., half:]
    return jnp.concatenate([t1 * cos - t2 * sin, t2 * cos + t1 * sin], axis=-1).astype(t.dtype)


def sliding_window_attention(q, k, v, sinks):
    B, S = q.shape[0], q.shape[1]
    nb = S // ATTN_BLOCK
    qb = q.reshape(B, nb, ATTN_BLOCK, N_KV_HEADS, Q_PER_KV, HEAD_DIM)

    def with_prev(t):
        tb = t.reshape(B, nb, ATTN_BLOCK, N_KV_HEADS, HEAD_DIM)
        prev = jnp.pad(tb[:, :-1], ((0, 0), (1, 0), (0, 0), (0, 0), (0, 0)))
        return jnp.concatenate([prev, tb], axis=2)

    kk, vv = with_prev(k), with_prev(v)
    s = jnp.einsum('bnqkgd,bnskd->bkgnqs', qb, kk).astype(jnp.float32) * (HEAD_DIM ** -0.5)
    blk = jnp.arange(nb)[:, None, None] * ATTN_BLOCK
    q_pos = blk + jnp.arange(ATTN_BLOCK)[None, :, None]
    k_pos = blk - ATTN_BLOCK + jnp.arange(2 * ATTN_BLOCK)[None, None, :]
    dist = q_pos - k_pos
    mask = (dist >= 0) & (dist < WINDOW) & (k_pos >= 0)
    s = jnp.where(mask, s, NEG_INF)
    sink = sinks.astype(jnp.float32).reshape(N_KV_HEADS, Q_PER_KV)[None, :, :, None, None, None]
    m = jnp.maximum(jnp.max(s, axis=-1, keepdims=True), sink)
    p = jnp.exp(s - m)
    p = p / (jnp.sum(p, axis=-1, keepdims=True) + jnp.exp(sink - m))
    o = jnp.einsum('bkgnqs,bnskd->bnqkgd', p.astype(v.dtype), vv)
    return o.reshape(B, S, ATTN_WIDTH)


def causal_depthwise_conv(u, w):
    S = u.shape[1]
    up = jnp.pad(u, ((0, 0), (CONV_K - 1, 0), (0, 0)))
    y = w[0] * up[:, 0:S]
    for i in range(1, CONV_K):
        y = y + w[i] * up[:, i:i + S]
    return y


def hybrid_mixer(h, cos, sin, w_in, sinks, conv_w, head_norm_w, w_out):
    B, S, _ = h.shape
    proj = h @ w_in
    q = apply_rope(proj[..., :Q_END].reshape(B, S, N_Q_HEADS, HEAD_DIM), cos, sin)
    k = apply_rope(proj[..., Q_END:K_END].reshape(B, S, N_KV_HEADS, HEAD_DIM), cos, sin)
    v = proj[..., K_END:V_END].reshape(B, S, N_KV_HEADS, HEAD_DIM)
    attn = sliding_window_attention(q, k, v, sinks)
    gate_b = proj[..., V_END:CB_END]
    gate_c = proj[..., CB_END:CC_END]
    hc = proj[..., CC_END:]
    conv = gate_b * causal_depthwise_conv(gate_c * hc, conv_w)
    y = jnp.concatenate([attn, conv], axis=-1).reshape(B, S, N_MIX_HEADS, HEAD_DIM)
    y = rms_norm(y, head_norm_w.reshape(N_MIX_HEADS, HEAD_DIM)).reshape(B, S, MIX_WIDTH)
    return y @ w_out


def swiglu(h, w_gate, w_up, w_down):
    return (jax.nn.silu(h @ w_gate) * (h @ w_up)) @ w_down


def moe_swiglu(h, router_w, w_gate, w_up, w_down):
    B, S, D = h.shape
    N = B * S
    t = h.reshape(N, D)
    logits = (t @ router_w).astype(jnp.float32)
    top_logits, top_idx = lax.top_k(logits, TOP_K)
    gates = jax.nn.softmax(top_logits, axis=-1)
    flat_e = top_idx.reshape(-1)
    flat_tok = jnp.repeat(jnp.arange(N, dtype=jnp.int32), TOP_K)
    flat_g = gates.reshape(-1)
    order = jnp.argsort(flat_e)
    sorted_e = flat_e[order]
    sizes = jnp.bincount(flat_e, length=N_EXPERTS)
    padded = (sizes + MOE_BLOCK - 1) // MOE_BLOCK * MOE_BLOCK
    group_start = jnp.cumsum(sizes) - sizes
    padded_end = jnp.cumsum(padded)
    padded_start = padded_end - padded
    n_assign = N * TOP_K
    rank = jnp.arange(n_assign, dtype=jnp.int32) - group_start[sorted_e]
    dest = padded_start[sorted_e] + rank
    n_rows = -(-n_assign // MOE_BLOCK) * MOE_BLOCK + N_EXPERTS * MOE_BLOCK
    n_blocks = n_rows // MOE_BLOCK
    row_tok = jnp.full((n_rows,), N, dtype=jnp.int32).at[dest].set(flat_tok[order])
    row_gate = jnp.zeros((n_rows,), jnp.float32).at[dest].set(flat_g[order])
    blk_e = jnp.minimum(jnp.searchsorted(padded_end, jnp.arange(n_blocks) * MOE_BLOCK, side='right'),
                        N_EXPERTS - 1)
    t_pad = jnp.concatenate([t, jnp.zeros((1, D), t.dtype)], axis=0)
    xs = t_pad[row_tok].reshape(n_blocks, MOE_BLOCK, D)

    def expert_block(args):
        xb, e = args
        return swiglu(xb, w_gate[e], w_up[e], w_down[e])

    ys = lax.map(expert_block, (xs, blk_e)).reshape(n_rows, D)
    ys = ys * row_gate[:, None].astype(ys.dtype)
    out = jnp.zeros((N + 1, D), ys.dtype).at[row_tok].add(ys)[:N]
    return out.reshape(B, S, D)


def setup_inputs(seed: int = 0) -> dict:
    key = jax.random.key(seed)
    ks = jax.random.split(key, 18)
    f32 = jnp.float32
    nrm = lambda k, shape, scale: jax.random.normal(k, shape, f32) * scale
    x = jax.random.normal(ks[0], (BATCH, SEQ, D_MODEL), f32)
    offset = jax.random.randint(ks[1], (BATCH,), 0, MAX_POS_OFFSET, dtype=jnp.int32)
    positions = offset[:, None] + jnp.arange(SEQ, dtype=jnp.int32)[None, :]
    return {
        'x': x,
        'positions': positions,
        'attn_norm_w': 1.0 + nrm(ks[2], (DEPTH, D_MODEL), 0.02),
        'w_in': nrm(ks[3], (DEPTH, D_MODEL, IN_WIDTH), D_MODEL ** -0.5),
        'attn_sinks': nrm(ks[4], (DEPTH, N_Q_HEADS), 0.5),
        'conv_w': nrm(ks[5], (DEPTH, CONV_K, CONV_WIDTH), CONV_K ** -0.5),
        'head_norm_w': 1.0 + nrm(ks[6], (DEPTH, MIX_WIDTH), 0.02),
        'w_out': nrm(ks[7], (DEPTH, MIX_WIDTH, D_MODEL), MIX_WIDTH ** -0.5),
        'ffn_norm_w': 1.0 + nrm(ks[8], (DEPTH, D_MODEL), 0.02),
        'dense_w_gate': nrm(ks[9], (N_DENSE, D_MODEL, D_FF), D_MODEL ** -0.5),
        'dense_w_up': nrm(ks[10], (N_DENSE, D_MODEL, D_FF), D_MODEL ** -0.5),
        'dense_w_down': nrm(ks[11], (N_DENSE, D_FF, D_MODEL), D_FF ** -0.5),
        'router_w': nrm(ks[12], (N_MOE, D_MODEL, N_EXPERTS), D_MODEL ** -0.5),
        'moe_w_gate': nrm(ks[13], (N_MOE, N_EXPERTS, D_MODEL, D_FF), D_MODEL ** -0.5),
        'moe_w_up': nrm(ks[14], (N_MOE, N_EXPERTS, D_MODEL, D_FF), D_MODEL ** -0.5),
        'moe_w_down': nrm(ks[15], (N_MOE, N_EXPERTS, D_FF, D_MODEL), D_FF ** -0.5),
        'final_norm_w': 1.0 + nrm(ks[16], (D_MODEL,), 0.02),
    }


def reference(x, positions, attn_norm_w, w_in, attn_sinks, conv_w, head_norm_w, w_out, ffn_norm_w,
              dense_w_gate, dense_w_up, dense_w_down, router_w, moe_w_gate, moe_w_up, moe_w_down,
              final_norm_w):
    cos, sin = rope_tables(positions)
    for l in range(DEPTH):
        h = rms_norm(x, attn_norm_w[l])
        x = x + hybrid_mixer(h, cos, sin, w_in[l], attn_sinks[l], conv_w[l], head_norm_w[l], w_out[l])
        h = rms_norm(x, ffn_norm_w[l])
        if l % 2 == 0:
            i = l // 2
            x = x + swiglu(h, dense_w_gate[i], dense_w_up[i], dense_w_down[i])
        else:
            i = l // 2
            x = x + moe_swiglu(h, router_w[i], moe_w_gate[i], moe_w_up[i], moe_w_down[i])
    return rms_norm(x, final_norm_w)
```

```python
import functools

import jax
import jax.numpy as jnp
from jax import lax
from jax.experimental import pallas as pl
from jax.experimental.pallas import tpu as pltpu

D_MODEL = 1024
HEAD_DIM = 64
N_Q_HEADS = 8
N_KV_HEADS = 2
ATTN_WIDTH = N_Q_HEADS * HEAD_DIM
KV_WIDTH = N_KV_HEADS * HEAD_DIM
CONV_WIDTH = D_MODEL - ATTN_WIDTH
Q_END = ATTN_WIDTH
K_END = Q_END + KV_WIDTH
V_END = K_END + KV_WIDTH
CB_END = V_END + CONV_WIDTH
CC_END = CB_END + CONV_WIDTH
IN_WIDTH = CC_END + CONV_WIDTH
WINDOW = 128
ROPE_THETA = 10000.0
CONV_K = 3
D_FF = 3584
N_EXPERTS = 8
TOP_K = 2
NORM_EPS = 1e-5
NEG_INF = -1e30

LANES = 128
SUBLANES = 8
MXU_DIM = 256
VMEM_LIMIT_BYTES = 58 * 1024 * 1024

SEQ_TILE = 512
FFN_TILE = 512
FF_CHUNK = 512
MOE_BLOCK = 256

F32 = jnp.float32
BF16 = jnp.bfloat16


def _rms(x, w):
    var = jnp.mean(x * x, axis=-1, keepdims=True)
    return x * lax.rsqrt(var + NORM_EPS) * w


def _dot(a, b):
    return jnp.dot(a, b, preferred_element_type=F32)


def _dot_nt(a, b):
    return lax.dot_general(a, b, (((1,), (1,)), ((), ())), preferred_element_type=F32)


def _mixer_kernel(ts, moe_in, router, *refs):
    refs = list(refs)
    x_ref = refs.pop(0)
    ya_ref = refs.pop(0) if moe_in else None
    yb_ref = refs.pop(0) if moe_in else None
    (cos_ref, sin_ref, anw_ref, win_ref, sinks_ref, convw_ref, hnw_ref, wout_ref, fnw_ref,
     gsum_ref) = refs[:10]
    refs = refs[10:]
    rw_ref = refs.pop(0) if router else None
    xo_ref = refs.pop(0)
    h2_ref = refs.pop(0)
    rt_ref = refs.pop(0) if router else None
    q_s, k_s, v_s, u_s, y_s = refs

    t = pl.program_id(1)
    nblk = ts // WINDOW

    @pl.when(t == 0)
    def _():
        k_s[:, 0:WINDOW, :] = jnp.zeros((4, WINDOW, LANES), BF16)
        v_s[:, 0:WINDOW, :] = jnp.zeros((4, WINDOW, LANES), BF16)
        u_s[0:SUBLANES, :] = jnp.zeros((SUBLANES, CONV_WIDTH), F32)

    @pl.when(t > 0)
    def _():
        k_s[:, 0:WINDOW, :] = k_s[:, ts:ts + WINDOW, :]
        v_s[:, 0:WINDOW, :] = v_s[:, ts:ts + WINDOW, :]
        u_s[0:SUBLANES, :] = u_s[ts:ts + SUBLANES, :]

    x = x_ref[...]
    if moe_in:
        x = x + ya_ref[...] + yb_ref[...]
    h = _rms(x, anw_ref[...]).astype(BF16)
    proj = _dot(h, win_ref[...])

    cosf = cos_ref[...]
    sinf = sin_ref[...]
    lane = lax.broadcasted_iota(jnp.int32, (ts, LANES), 1)
    first_half = (lane & (HEAD_DIM // 2)) == 0
    low_head = lane < HEAD_DIM

    def rope(tq):
        rot = jnp.where(first_half, pltpu.roll(tq, LANES - HEAD_DIM // 2, 1), pltpu.roll(tq, HEAD_DIM // 2, 1))
        return tq * cosf + rot * sinf

    scale = HEAD_DIM ** -0.5
    for p in range(ATTN_WIDTH // LANES):
        q_s[:, p * LANES:(p + 1) * LANES] = (rope(proj[:, p * LANES:(p + 1) * LANES]) * scale).astype(BF16)

    def lane_variants(a, dst):
        ar = pltpu.roll(a, HEAD_DIM, 1)
        zero = jnp.zeros_like(a)
        dst[0, WINDOW:WINDOW + ts, :] = jnp.where(low_head, a, zero).astype(BF16)
        dst[1, WINDOW:WINDOW + ts, :] = jnp.where(low_head, zero, ar).astype(BF16)
        dst[2, WINDOW:WINDOW + ts, :] = jnp.where(low_head, ar, zero).astype(BF16)
        dst[3, WINDOW:WINDOW + ts, :] = jnp.where(low_head, zero, a).astype(BF16)

    lane_variants(rope(proj[:, Q_END:K_END]), k_s)
    lane_variants(proj[:, K_END:V_END], v_s)

    qi = lax.broadcasted_iota(jnp.int32, (WINDOW, 2 * WINDOW), 0)
    kj = lax.broadcasted_iota(jnp.int32, (WINDOW, 2 * WINDOW), 1)
    band = (kj > qi) & (kj <= qi + WINDOW)
    band_first = band & (kj >= jnp.where(t > 0, 0, WINDOW))

    for b in range(nblk):
        mask = band_first if b == 0 else band
        rows = slice(b * WINDOW, (b + 1) * WINDOW)
        ctx = slice(b * WINDOW, (b + 2) * WINDOW)
        for g in range(N_KV_HEADS):
            qg = jnp.concatenate([q_s[rows, (2 * g) * LANES:(2 * g + 1) * LANES],
                                  q_s[rows, (2 * g + 1) * LANES:(2 * g + 2) * LANES]], axis=0)
            probs = []
            for j in range(2):
                s = _dot_nt(qg, k_s[2 * g + j, ctx, :])
                halves = []
                for r in range(2):
                    sink = sinks_ref[4 * g + 2 * r + j]
                    sr = jnp.where(mask, s[r * WINDOW:(r + 1) * WINDOW], NEG_INF)
                    m = jnp.maximum(jnp.max(sr, axis=-1, keepdims=True), sink)
                    e = jnp.exp(sr - m)
                    denom = jnp.sum(e, axis=-1, keepdims=True) + jnp.exp(sink - m)
                    halves.append((e * (1.0 / denom)).astype(BF16))
                probs.append(jnp.concatenate(halves, axis=0))
            o = _dot(probs[0], v_s[2 * g, ctx, :]) + _dot(probs[1], v_s[2 * g + 1, ctx, :])
            y_s[rows, (2 * g) * LANES:(2 * g + 1) * LANES] = o[0:WINDOW]
            y_s[rows, (2 * g + 1) * LANES:(2 * g + 2) * LANES] = o[WINDOW:2 * WINDOW]

    u_s[SUBLANES:SUBLANES + ts, :] = proj[:, CB_END:CC_END] * proj[:, CC_END:IN_WIDTH]
    cw = convw_ref[...]
    conv = cw[0:1, :] * u_s[SUBLANES - 2:SUBLANES - 2 + ts, :]
    conv = conv + cw[1:2, :] * u_s[SUBLANES - 1:SUBLANES - 1 + ts, :]
    conv = conv + cw[2:3, :] * u_s[SUBLANES:SUBLANES + ts, :]
    y_s[:, ATTN_WIDTH:D_MODEL] = proj[:, V_END:CB_END] * conv

    y = y_s[...]
    sq = (y * y).astype(BF16)
    gsum = gsum_ref[...]
    ms = jnp.concatenate([_dot(sq[:, c * MXU_DIM:(c + 1) * MXU_DIM], gsum) for c in range(D_MODEL // MXU_DIM)],
                         axis=-1)
    yn = (y * lax.rsqrt(ms + NORM_EPS) * hnw_ref[...]).astype(BF16)

    xn = x + _dot(yn, wout_ref[...])
    xo_ref[...] = xn
    h2 = _rms(xn, fnw_ref[...])
    h2_ref[...] = h2.astype(h2_ref.dtype)

    if router:
        lg = _dot(h2.astype(BF16), rw_ref[...])
        lanef = lane.astype(F32)
        l1 = jnp.where(lane < N_EXPERTS, lg, -jnp.inf)
        m1 = jnp.max(l1, axis=-1, keepdims=True)
        i1 = jnp.min(jnp.where(l1 == m1, lanef, float(LANES)), axis=-1, keepdims=True)
        l2 = jnp.where(lanef == i1, -jnp.inf, l1)
        m2 = jnp.max(l2, axis=-1, keepdims=True)
        i2 = jnp.min(jnp.where(l2 == m2, lanef, float(LANES)), axis=-1, keepdims=True)
        e2 = jnp.exp(m2 - m1)
        inv = 1.0 / (1.0 + e2)
        out = jnp.where(lane == 0, i1, jnp.where(lane == 1, i2, jnp.where(lane == 2, inv, e2 * inv)))
        rt_ref[...] = jnp.where(lane < 4, out, 0.0)


def _resident(shape):
    return pl.BlockSpec(shape, lambda *_: (0,) * len(shape), pipeline_mode=pl.Buffered(1))


def _mixer(x, ymoe, cosf, sinf, anw, win, sinks, convw, hnw, wout, fnw, gsum, rw, seq, h2_dtype):
    n = x.shape[0]
    ts = min(SEQ_TILE, seq)
    nt = seq // ts
    nb = n // seq
    moe_in = ymoe is not None
    router = rw is not None
    row = lambda b, t: (b * nt + t, 0)
    tile = pl.BlockSpec((ts, D_MODEL), row)
    in_specs = [tile]
    args = [x]
    if moe_in:
        off = n // ts
        in_specs += [tile, pl.BlockSpec((ts, D_MODEL), lambda b, t: (off + b * nt + t, 0))]
        args += [ymoe, ymoe]
    in_specs += [pl.BlockSpec((ts, LANES), row), pl.BlockSpec((ts, LANES), row),
                 _resident((1, D_MODEL)), _resident((D_MODEL, IN_WIDTH)),
                 pl.BlockSpec(memory_space=pltpu.SMEM),
                 _resident((CONV_K, CONV_WIDTH)), _resident((1, D_MODEL)), _resident((D_MODEL, D_MODEL)),
                 _resident((1, D_MODEL)), _resident((MXU_DIM, MXU_DIM))]
    args += [cosf, sinf, anw, win, sinks, convw, hnw, wout, fnw, gsum]
    out_shape = [jax.ShapeDtypeStruct((n, D_MODEL), F32), jax.ShapeDtypeStruct((n, D_MODEL), h2_dtype)]
    out_specs = [tile, tile]
    if router:
        in_specs.append(_resident((D_MODEL, LANES)))
        args.append(rw)
        out_shape.append(jax.ShapeDtypeStruct((n, LANES), F32))
        out_specs.append(pl.BlockSpec((ts, LANES), row))
    return pl.pallas_call(
        functools.partial(_mixer_kernel, ts, moe_in, router),
        grid=(nb, nt),
        in_specs=in_specs,
        out_specs=out_specs,
        out_shape=out_shape,
        scratch_shapes=[pltpu.VMEM((ts, ATTN_WIDTH), BF16),
                        pltpu.VMEM((4, WINDOW + ts, LANES), BF16),
                        pltpu.VMEM((4, WINDOW + ts, LANES), BF16),
                        pltpu.VMEM((SUBLANES + ts, CONV_WIDTH), F32),
                        pltpu.VMEM((ts, D_MODEL), F32)],
        compiler_params=pltpu.CompilerParams(dimension_semantics=("arbitrary", "arbitrary"),
                                             vmem_limit_bytes=VMEM_LIMIT_BYTES),
        name="mixer",
    )(*args)


def _swiglu(hb, wg_ref, wu_ref, wd_ref, acc_ref):
    for c in range(D_FF // FF_CHUNK):
        cols = slice(c * FF_CHUNK, (c + 1) * FF_CHUNK)
        g = _dot(hb, wg_ref[:, cols])
        u = _dot(hb, wu_ref[:, cols])
        a = (jax.nn.silu(g) * u).astype(BF16)
        part = _dot(a, wd_ref[cols, :])
        if c == 0:
            acc_ref[...] = part
        else:
            acc_ref[...] += part


def _ffn_kernel(x_ref, h_ref, wg_ref, wu_ref, wd_ref, o_ref, acc_ref):
    _swiglu(h_ref[...], wg_ref, wu_ref, wd_ref, acc_ref)
    o_ref[...] = x_ref[...] + acc_ref[...]


def _dense_ffn(x, h2, wg, wu, wd):
    n = x.shape[0]
    tm = min(FFN_TILE, n)
    tile = pl.BlockSpec((tm, D_MODEL), lambda i: (i, 0))
    return pl.pallas_call(
        _ffn_kernel,
        grid=(n // tm,),
        in_specs=[tile, tile, _resident((D_MODEL, D_FF)), _resident((D_MODEL, D_FF)), _resident((D_FF, D_MODEL))],
        out_specs=tile,
        out_shape=jax.ShapeDtypeStruct((n, D_MODEL), F32),
        scratch_shapes=[pltpu.VMEM((tm, D_MODEL), F32)],
        compiler_params=pltpu.CompilerParams(dimension_semantics=("arbitrary",),
                                             vmem_limit_bytes=VMEM_LIMIT_BYTES),
        name="dense_ffn",
    )(x, h2, wg, wu, wd)


def _moe_kernel(blk_e_ref, src_ref, dst_ref, gate_ref, h_hbm, wg_ref, wu_ref, wd_ref, y_hbm,
                xbuf, ybuf, acc_ref, gsem, ssem):
    del blk_e_ref
    i = pl.program_id(0)
    nblocks = pl.num_programs(0)
    slot = i % 2
    bm = MOE_BLOCK

    def gather(block_slot, which):
        def body(r, c):
            tok = src_ref[which, 0, r]
            pltpu.make_async_copy(h_hbm.at[pl.ds(tok, 1)], xbuf.at[block_slot, pl.ds(r, 1)],
                                  gsem.at[block_slot]).start()
            return c
        lax.fori_loop(0, bm, body, 0)

    def wait_gather(block_slot):
        pltpu.make_async_copy(h_hbm.at[pl.ds(0, bm)], xbuf.at[block_slot], gsem.at[block_slot]).wait()

    def wait_scatter(block_slot):
        pltpu.make_async_copy(ybuf.at[block_slot], y_hbm.at[pl.ds(0, bm)], ssem.at[block_slot]).wait()

    @pl.when(i == 0)
    def _():
        gather(0, 0)

    @pl.when(i + 1 < nblocks)
    def _():
        gather(1 - slot, 1)

    wait_gather(slot)
    _swiglu(xbuf[slot].astype(BF16), wg_ref, wu_ref, wd_ref, acc_ref)

    @pl.when(i >= 2)
    def _():
        wait_scatter(slot)

    ybuf[slot] = acc_ref[...] * gate_ref[0]

    def scatter(r, c):
        row = dst_ref[0, 0, r]
        pltpu.make_async_copy(ybuf.at[slot, pl.ds(r, 1)], y_hbm.at[pl.ds(row, 1)], ssem.at[slot]).start()
        return c
    lax.fori_loop(0, bm, scatter, 0)

    @pl.when(i == nblocks - 1)
    def _():
        wait_scatter(slot)

        @pl.when(nblocks >= 2)
        def _():
            wait_scatter(1 - slot)


def _moe_experts(h2, blk_e, src2, dst, gate, wg, wu, wd):
    n_rows = dst.shape[0] * MOE_BLOCK
    nblocks = dst.shape[0]
    bm = MOE_BLOCK
    wspec_in = pl.BlockSpec((None, D_MODEL, D_FF), lambda i, be: (be[i], 0, 0), pipeline_mode=pl.Buffered(1))
    wspec_out = pl.BlockSpec((None, D_FF, D_MODEL), lambda i, be: (be[i], 0, 0), pipeline_mode=pl.Buffered(1))
    grid_spec = pltpu.PrefetchScalarGridSpec(
        num_scalar_prefetch=1,
        grid=(nblocks,),
        in_specs=[pl.BlockSpec((2, 1, bm), lambda i, be: (i, 0, 0), memory_space=pltpu.SMEM),
                  pl.BlockSpec((1, 1, bm), lambda i, be: (i, 0, 0), memory_space=pltpu.SMEM),
                  pl.BlockSpec((1, bm, 1), lambda i, be: (i, 0, 0)),
                  pl.BlockSpec(memory_space=pl.ANY),
                  wspec_in, wspec_in, wspec_out],
        out_specs=pl.BlockSpec(memory_space=pl.ANY),
        scratch_shapes=[pltpu.VMEM((2, bm, D_MODEL), F32),
                        pltpu.VMEM((2, bm, D_MODEL), F32),
                        pltpu.VMEM((bm, D_MODEL), F32),
                        pltpu.SemaphoreType.DMA((2,)),
                        pltpu.SemaphoreType.DMA((2,))],
    )
    return pl.pallas_call(
        _moe_kernel,
        grid_spec=grid_spec,
        out_shape=jax.ShapeDtypeStruct((n_rows, D_MODEL), F32),
        compiler_params=pltpu.CompilerParams(dimension_semantics=("arbitrary",),
                                             vmem_limit_bytes=VMEM_LIMIT_BYTES),
        name="moe_experts",
    )(blk_e, src2, dst, gate, h2, wg, wu, wd)


def _route(rt, n):
    bm = MOE_BLOCK
    top_idx = rt[:, 0:TOP_K].astype(jnp.int32)
    gates = rt[:, TOP_K:2 * TOP_K]
    n_assign = n * TOP_K
    n_rows = n_assign + N_EXPERTS * bm
    nblocks = n_rows // bm
    flat_e = top_idx.reshape(-1)
    order = jnp.argsort(flat_e)
    sorted_e = flat_e[order]
    sizes = jnp.bincount(flat_e, length=N_EXPERTS)
    padded = (sizes + bm - 1) // bm * bm
    group_start = jnp.cumsum(sizes) - sizes
    padded_end = jnp.cumsum(padded)
    padded_start = padded_end - padded
    rank = jnp.arange(n_assign, dtype=jnp.int32) - group_start[sorted_e]
    slot = (padded_start[sorted_e] + rank).astype(jnp.int32)
    assign = jnp.full((n_rows,), -1, jnp.int32).at[slot].set(order.astype(jnp.int32))
    is_pad = assign < 0
    a = jnp.maximum(assign, 0)
    tok = a // TOP_K
    k = a % TOP_K
    src = jnp.where(is_pad, 0, tok)
    pad_rank = jnp.cumsum(is_pad.astype(jnp.int32)) - 1
    dst = jnp.where(is_pad, n_assign + pad_rank, k * n + tok)
    gate = jnp.where(is_pad, 0.0, gates.reshape(-1)[a])
    blk_e = jnp.minimum(jnp.searchsorted(padded_end, jnp.arange(nblocks) * bm, side='right'),
                        N_EXPERTS - 1).astype(jnp.int32)
    src_b = src.reshape(nblocks, 1, bm)
    src_next = jnp.concatenate([src_b[1:], src_b[-1:]], axis=0)
    src2 = jnp.stack([src_b, src_next], axis=1).reshape(nblocks * 2, 1, bm)
    return blk_e, src2, dst.reshape(nblocks, 1, bm), gate.reshape(nblocks, bm, 1)


def _final_kernel(x_ref, ya_ref, yb_ref, w_ref, o_ref):
    o_ref[...] = _rms(x_ref[...] + ya_ref[...] + yb_ref[...], w_ref[...])


def _final(x, ymoe, w):
    n = x.shape[0]
    tm = min(1024, n)
    off = n // tm
    tile = pl.BlockSpec((tm, D_MODEL), lambda i: (i, 0))
    return pl.pallas_call(
        _final_kernel,
        grid=(n // tm,),
        in_specs=[tile, tile, pl.BlockSpec((tm, D_MODEL), lambda i: (off + i, 0)), _resident((1, D_MODEL))],
        out_specs=tile,
        out_shape=jax.ShapeDtypeStruct((n, D_MODEL), F32),
        compiler_params=pltpu.CompilerParams(dimension_semantics=("arbitrary",)),
        name="final_norm",
    )(x, ymoe, ymoe, w)


def kernel(x, positions, attn_norm_w, w_in, attn_sinks, conv_w, head_norm_w, w_out, ffn_norm_w, dense_w_gate,
           dense_w_up, dense_w_down, router_w, moe_w_gate, moe_w_up, moe_w_down, final_norm_w):
    bsz, seq, _ = x.shape
    n = bsz * seq
    depth = w_in.shape[0]

    inv_freq = ROPE_THETA ** (-jnp.arange(0, HEAD_DIM, 2, dtype=F32) / HEAD_DIM)
    ang = positions.astype(F32)[..., None] * inv_freq
    cos, sin = jnp.cos(ang), jnp.sin(ang)
    cosf = jnp.tile(cos, (1, 1, 4)).reshape(n, LANES)
    sinf = jnp.tile(jnp.concatenate([-sin, sin], axis=-1), (1, 1, 2)).reshape(n, LANES)

    grp = jnp.arange(MXU_DIM) // HEAD_DIM
    gsum = jnp.where(grp[:, None] == grp[None, :], 1.0 / HEAD_DIM, 0.0).astype(BF16)

    xf = x.reshape(n, D_MODEL)
    ymoe = None
    for l in range(depth):
        is_moe = l % 2 == 1
        i = l // 2
        rw = None
        if is_moe:
            rw = jnp.pad(router_w[i], ((0, 0), (0, LANES - N_EXPERTS))).astype(BF16)
        outs = _mixer(xf, ymoe, cosf, sinf, attn_norm_w[l][None], w_in[l].astype(BF16), attn_sinks[l],
                      conv_w[l], head_norm_w[l][None], w_out[l].astype(BF16), ffn_norm_w[l][None], gsum, rw,
                      seq, F32 if is_moe else BF16)
        if is_moe:
            xf, h2, rt = outs
            blk_e, src2, dst, gate = _route(rt, n)
            ymoe = _moe_experts(h2, blk_e, src2, dst, gate, moe_w_gate[i].astype(BF16),
                                moe_w_up[i].astype(BF16), moe_w_down[i].astype(BF16))
        else:
            xf, h2 = outs
            xf = _dense_ffn(xf, h2, dense_w_gate[i].astype(BF16), dense_w_up[i].astype(BF16),
                            dense_w_down[i].astype(BF16))
            ymoe = None
    if ymoe is None:
        ymoe = jnp.zeros((2 * n, D_MODEL), F32)
    out = _final(xf, ymoe, final_norm_w[None])
    return out.reshape(bsz, seq, D_MODEL)
```

```python
import functools

import jax
import jax.numpy as jnp
from jax import lax
from jax.experimental import pallas as pl
from jax.experimental.pallas import tpu as pltpu

D_MODEL = 1024
HEAD_DIM = 64
N_Q_HEADS = 8
N_KV_HEADS = 2
ATTN_WIDTH = N_Q_HEADS * HEAD_DIM
KV_WIDTH = N_KV_HEADS * HEAD_DIM
CONV_WIDTH = D_MODEL - ATTN_WIDTH
Q_END = ATTN_WIDTH
K_END = Q_END + KV_WIDTH
V_END = K_END + KV_WIDTH
CB_END = V_END + CONV_WIDTH
CC_END = CB_END + CONV_WIDTH
IN_WIDTH = CC_END + CONV_WIDTH
WINDOW = 128
ROPE_THETA = 10000.0
CONV_K = 3
D_FF = 3584
N_EXPERTS = 8
TOP_K = 2
NORM_EPS = 1e-5
NEG_INF = -1e30

LANES = 128
SUBLANES = 8
MXU_DIM = 256
VMEM_LIMIT_BYTES = 58 * 1024 * 1024

SEQ_TILE = 512
FFN_TILE = 512
FF_CHUNK = 512
MOE_BLOCK = 256

F32 = jnp.float32
BF16 = jnp.bfloat16


def _rms(x, w):
    var = jnp.mean(x * x, axis=-1, keepdims=True)
    return x * lax.rsqrt(var + NORM_EPS) * w


def _dot(a, b):
    return jnp.dot(a, b, preferred_element_type=F32)


def _dot_nt(a, b):
    return lax.dot_general(a, b, (((1,), (1,)), ((), ())), preferred_element_type=F32)


def _mixer_kernel(ts, moe_in, router, *refs):
    refs = list(refs)
    x_ref = refs.pop(0)
    ya_ref = refs.pop(0) if moe_in else None
    yb_ref = refs.pop(0) if moe_in else None
    (cos_ref, sin_ref, anw_ref, win_ref, sinks_ref, convw_ref, hnw_ref, wout_ref, fnw_ref,
     gsum_ref) = refs[:10]
    refs = refs[10:]
    rw_ref = refs.pop(0) if router else None
    xo_ref = refs.pop(0)
    h2_ref = refs.pop(0)
    rt_ref = refs.pop(0) if router else None
    q_s, k_s, v_s, u_s, y_s = refs

    t = pl.program_id(1)
    nblk = ts // WINDOW

    @pl.when(t == 0)
    def _():
        k_s[:, 0:WINDOW, :] = jnp.zeros((4, WINDOW, LANES), BF16)
        v_s[:, 0:WINDOW, :] = jnp.zeros((4, WINDOW, LANES), BF16)
        u_s[0:SUBLANES, :] = jnp.zeros((SUBLANES, CONV_WIDTH), F32)

    @pl.when(t > 0)
    def _():
        k_s[:, 0:WINDOW, :] = k_s[:, ts:ts + WINDOW, :]
        v_s[:, 0:WINDOW, :] = v_s[:, ts:ts + WINDOW, :]
        u_s[0:SUBLANES, :] = u_s[ts:ts + SUBLANES, :]

    x = x_ref[...]
    if moe_in:
        x = x + ya_ref[...] + yb_ref[...]
    h = _rms(x, anw_ref[...]).astype(BF16)
    proj = _dot(h, win_ref[...])

    cosf = cos_ref[...]
    sinf = sin_ref[...]
    lane = lax.broadcasted_iota(jnp.int32, (ts, LANES), 1)
    first_half = (lane & (HEAD_DIM // 2)) == 0
    low_head = lane < HEAD_DIM

    def rope(tq):
        rot = jnp.where(first_half, pltpu.roll(tq, LANES - HEAD_DIM // 2, 1), pltpu.roll(tq, HEAD_DIM // 2, 1))
        return tq * cosf + rot * sinf

    scale = HEAD_DIM ** -0.5
    for p in range(ATTN_WIDTH // LANES):
        q_s[:, p * LANES:(p + 1) * LANES] = (rope(proj[:, p * LANES:(p + 1) * LANES]) * scale).astype(BF16)

    def lane_variants(a, dst):
        ar = pltpu.roll(a, HEAD_DIM, 1)
        zero = jnp.zeros_like(a)
        dst[0, WINDOW:WINDOW + ts, :] = jnp.where(low_head, a, zero).astype(BF16)
        dst[1, WINDOW:WINDOW + ts, :] = jnp.where(low_head, zero, ar).astype(BF16)
        dst[2, WINDOW:WINDOW + ts, :] = jnp.where(low_head, ar, zero).astype(BF16)
        dst[3, WINDOW:WINDOW + ts, :] = jnp.where(low_head, zero, a).astype(BF16)

    lane_variants(rope(proj[:, Q_END:K_END]), k_s)
    lane_variants(proj[:, K_END:V_END], v_s)

    qi = lax.broadcasted_iota(jnp.int32, (WINDOW, 2 * WINDOW), 0)
    kj = lax.broadcasted_iota(jnp.int32, (WINDOW, 2 * WINDOW), 1)
    band = (kj > qi) & (kj <= qi + WINDOW)
    band_first = band & (kj >= jnp.where(t > 0, 0, WINDOW))

    for b in range(nblk):
        mask = band_first if b == 0 else band
        rows = slice(b * WINDOW, (b + 1) * WINDOW)
        ctx = slice(b * WINDOW, (b + 2) * WINDOW)
        for g in range(N_KV_HEADS):
            qg = jnp.concatenate([q_s[rows, (2 * g) * LANES:(2 * g + 1) * LANES],
                                  q_s[rows, (2 * g + 1) * LANES:(2 * g + 2) * LANES]], axis=0)
            probs = []
            for j in range(2):
                s = _dot_nt(qg, k_s[2 * g + j, ctx, :])
                halves = []
                for r in range(2):
                    sink = sinks_ref[4 * g + 2 * r + j]
                    sr = jnp.where(mask, s[r * WINDOW:(r + 1) * WINDOW], NEG_INF)
                    m = jnp.maximum(jnp.max(sr, axis=-1, keepdims=True), sink)
                    e = jnp.exp(sr - m)
                    denom = jnp.sum(e, axis=-1, keepdims=True) + jnp.exp(sink - m)
                    halves.append((e * (1.0 / denom)).astype(BF16))
                probs.append(jnp.concatenate(halves, axis=0))
            o = _dot(probs[0], v_s[2 * g, ctx, :]) + _dot(probs[1], v_s[2 * g + 1, ctx, :])
            y_s[rows, (2 * g) * LANES:(2 * g + 1) * LANES] = o[0:WINDOW]
            y_s[rows, (2 * g + 1) * LANES:(2 * g + 2) * LANES] = o[WINDOW:2 * WINDOW]

    u_s[SUBLANES:SUBLANES + ts, :] = proj[:, CB_END:CC_END] * proj[:, CC_END:IN_WIDTH]
    cw = convw_ref[...]
    conv = cw[0:1, :] * u_s[SUBLANES - 2:SUBLANES - 2 + ts, :]
    conv = conv + cw[1:2, :] * u_s[SUBLANES - 1:SUBLANES - 1 + ts, :]
    conv = conv + cw[2:3, :] * u_s[SUBLANES:SUBLANES + ts, :]
    y_s[:, ATTN_WIDTH:D_MODEL] = proj[:, V_END:CB_END] * conv

    y = y_s[...]
    sq = (y * y).astype(BF16)
    gsum = gsum_ref[...]
    ms = jnp.concatenate([_dot(sq[:, c * MXU_DIM:(c + 1) * MXU_DIM], gsum) for c in range(D_MODEL // MXU_DIM)],
                         axis=-1)
    yn = (y * lax.rsqrt(ms + NORM_EPS) * hnw_ref[...]).astype(BF16)

    xn = x + _dot(yn, wout_ref[...])
    xo_ref[...] = xn
    h2 = _rms(xn, fnw_ref[...])
    h2_ref[...] = h2.astype(h2_ref.dtype)

    if router:
        lg = _dot(h2.astype(BF16), rw_ref[...])
        lanef = lane.astype(F32)
        l1 = jnp.where(lane < N_EXPERTS, lg, -jnp.inf)
        m1 = jnp.max(l1, axis=-1, keepdims=True)
        i1 = jnp.min(jnp.where(l1 == m1, lanef, float(LANES)), axis=-1, keepdims=True)
        l2 = jnp.where(lanef == i1, -jnp.inf, l1)
        m2 = jnp.max(l2, axis=-1, keepdims=True)
        i2 = jnp.min(jnp.where(l2 == m2, lanef, float(LANES)), axis=-1, keepdims=True)
        e2 = jnp.exp(m2 - m1)
        inv = 1.0 / (1.0 + e2)
        out = jnp.where(lane == 0, i1, jnp.where(lane == 1, i2, jnp.where(lane == 2, inv, e2 * inv)))
        rt_ref[...] = jnp.where(lane < 4, out, 0.0)


def _resident(shape):
    return pl.BlockSpec(shape, lambda *_: (0,) * len(shape), pipeline_mode=pl.Buffered(1))


def _mixer(x, ymoe, cosf, sinf, anw, win, sinks, convw, hnw, wout, fnw, gsum, rw, seq, h2_dtype):
    n = x.shape[0]
    ts = min(SEQ_TILE, seq)
    nt = seq // ts
    nb = n // seq
    moe_in = ymoe is not None
    router = rw is not None
    row = lambda b, t: (b * nt + t, 0)
    tile = pl.BlockSpec((ts, D_MODEL), row)
    in_specs = [tile]
    args = [x]
    if moe_in:
        off = n // ts
        in_specs += [tile, pl.BlockSpec((ts, D_MODEL), lambda b, t: (off + b * nt + t, 0))]
        args += [ymoe, ymoe]
    in_specs += [pl.BlockSpec((ts, LANES), row), pl.BlockSpec((ts, LANES), row),
                 _resident((1, D_MODEL)), _resident((D_MODEL, IN_WIDTH)),
                 pl.BlockSpec(memory_space=pltpu.SMEM),
                 _resident((CONV_K, CONV_WIDTH)), _resident((1, D_MODEL)), _resident((D_MODEL, D_MODEL)),
                 _resident((1, D_MODEL)), _resident((MXU_DIM, MXU_DIM))]
    args += [cosf, sinf, anw, win, sinks, convw, hnw, wout, fnw, gsum]
    out_shape = [jax.ShapeDtypeStruct((n, D_MODEL), F32), jax.ShapeDtypeStruct((n, D_MODEL), h2_dtype)]
    out_specs = [tile, tile]
    if router:
        in_specs.append(_resident((D_MODEL, LANES)))
        args.append(rw)
        out_shape.append(jax.ShapeDtypeStruct((n, LANES), F32))
        out_specs.append(pl.BlockSpec((ts, LANES), row))
    return pl.pallas_call(
        functools.partial(_mixer_kernel, ts, moe_in, router),
        grid=(nb, nt),
        in_specs=in_specs,
        out_specs=out_specs,
        out_shape=out_shape,
        scratch_shapes=[pltpu.VMEM((ts, ATTN_WIDTH), BF16),
                        pltpu.VMEM((4, WINDOW + ts, LANES), BF16),
                        pltpu.VMEM((4, WINDOW + ts, LANES), BF16),
                        pltpu.VMEM((SUBLANES + ts, CONV_WIDTH), F32),
                        pltpu.VMEM((ts, D_MODEL), F32)],
        compiler_params=pltpu.CompilerParams(dimension_semantics=("arbitrary", "arbitrary"),
                                             vmem_limit_bytes=VMEM_LIMIT_BYTES),
        name="mixer",
    )(*args)


def _swiglu(hb, wg_ref, wu_ref, wd_ref, acc_ref, side_work=None):
    for c in range(D_FF // FF_CHUNK):
        if side_work is not None:
            side_work(c)
        cols = slice(c * FF_CHUNK, (c + 1) * FF_CHUNK)
        g = _dot(hb, wg_ref[:, cols])
        u = _dot(hb, wu_ref[:, cols])
        a = (jax.nn.silu(g) * u).astype(BF16)
        part = _dot(a, wd_ref[cols, :])
        if c == 0:
            acc_ref[...] = part
        else:
            acc_ref[...] += part


def _ffn_kernel(x_ref, h_ref, wg_ref, wu_ref, wd_ref, o_ref, acc_ref):
    _swiglu(h_ref[...], wg_ref, wu_ref, wd_ref, acc_ref)
    o_ref[...] = x_ref[...] + acc_ref[...]


def _dense_ffn(x, h2, wg, wu, wd):
    n = x.shape[0]
    tm = min(FFN_TILE, n)
    tile = pl.BlockSpec((tm, D_MODEL), lambda i: (i, 0))
    return pl.pallas_call(
        _ffn_kernel,
        grid=(n // tm,),
        in_specs=[tile, tile, _resident((D_MODEL, D_FF)), _resident((D_MODEL, D_FF)), _resident((D_FF, D_MODEL))],
        out_specs=tile,
        out_shape=jax.ShapeDtypeStruct((n, D_MODEL), F32),
        scratch_shapes=[pltpu.VMEM((tm, D_MODEL), F32)],
        compiler_params=pltpu.CompilerParams(dimension_semantics=("arbitrary",),
                                             vmem_limit_bytes=VMEM_LIMIT_BYTES),
        name="dense_ffn",
    )(x, h2, wg, wu, wd)


def _moe_kernel(blk_e_ref, src0_ref, srcn_ref, dstp_ref, dstl_ref, gate_ref, h_hbm, wg_ref, wu_ref, wd_ref, y_hbm,
                xbuf, ybuf, acc_ref, gsem, ssem):
    del blk_e_ref
    i = pl.program_id(0)
    last = pl.num_programs(0) - 1
    slot = i % 2
    other = 1 - slot
    bm = MOE_BLOCK
    nchunks = D_FF // FF_CHUNK
    per_chunk = -(-bm // nchunks)

    def gather_row(idx_ref, r, block_slot):
        pltpu.make_async_copy(h_hbm.at[pl.ds(idx_ref[0, 0, r], 1)], xbuf.at[block_slot, pl.ds(r, 1)],
                              gsem.at[block_slot]).start()

    def scatter_row(idx_ref, r, block_slot):
        pltpu.make_async_copy(ybuf.at[block_slot, pl.ds(r, 1)], y_hbm.at[pl.ds(idx_ref[0, 0, r], 1)],
                              ssem.at[block_slot]).start()

    def wait_gather(block_slot):
        pltpu.make_async_copy(h_hbm.at[pl.ds(0, bm)], xbuf.at[block_slot], gsem.at[block_slot]).wait()

    def wait_scatter(block_slot):
        pltpu.make_async_copy(ybuf.at[block_slot], y_hbm.at[pl.ds(0, bm)], ssem.at[block_slot]).wait()

    @pl.when(i == 0)
    def _():
        def body(r, c):
            gather_row(src0_ref, r, 0)
            return c
        lax.fori_loop(0, bm, body, 0)
        ybuf[1] = jnp.zeros((bm, D_MODEL), F32)

    wait_gather(slot)

    def side_work(c):
        for r in range(c * per_chunk, min((c + 1) * per_chunk, bm)):
            gather_row(srcn_ref, r, other)
            scatter_row(dstp_ref, r, other)

    _swiglu(xbuf[slot].astype(BF16), wg_ref, wu_ref, wd_ref, acc_ref, side_work)

    @pl.when(i >= 1)
    def _():
        wait_scatter(slot)

    ybuf[slot] = acc_ref[...] * gate_ref[0]

    @pl.when(i == last)
    def _():
        def body(r, c):
            scatter_row(dstl_ref, r, slot)
            return c
        lax.fori_loop(0, bm, body, 0)
        wait_scatter(slot)
        wait_scatter(other)
        wait_gather(other)


def _moe_experts(h2, blk_e, src, dst, gate, wg, wu, wd):
    nblocks = src.shape[0]
    bm = MOE_BLOCK
    n_rows = (nblocks + 1) * bm
    wspec_in = pl.BlockSpec((None, D_MODEL, D_FF), lambda i, be: (be[i], 0, 0), pipeline_mode=pl.Buffered(1))
    wspec_out = pl.BlockSpec((None, D_FF, D_MODEL), lambda i, be: (be[i], 0, 0), pipeline_mode=pl.Buffered(1))
    smem_rows = lambda index_map: pl.BlockSpec((1, 1, bm), index_map, memory_space=pltpu.SMEM)
    grid_spec = pltpu.PrefetchScalarGridSpec(
        num_scalar_prefetch=1,
        grid=(nblocks,),
        in_specs=[smem_rows(lambda i, be: (0, 0, 0)),
                  smem_rows(lambda i, be: (jnp.minimum(i + 1, nblocks - 1), 0, 0)),
                  smem_rows(lambda i, be: (i, 0, 0)),
                  smem_rows(lambda i, be: (nblocks, 0, 0)),
                  pl.BlockSpec((1, bm, 1), lambda i, be: (i, 0, 0)),
                  pl.BlockSpec(memory_space=pl.ANY),
                  wspec_in, wspec_in, wspec_out],
        out_specs=pl.BlockSpec(memory_space=pl.ANY),
        scratch_shapes=[pltpu.VMEM((2, bm, D_MODEL), F32),
                        pltpu.VMEM((2, bm, D_MODEL), F32),
                        pltpu.VMEM((bm, D_MODEL), F32),
                        pltpu.SemaphoreType.DMA((2,)),
                        pltpu.SemaphoreType.DMA((2,))],
    )
    return pl.pallas_call(
        _moe_kernel,
        grid_spec=grid_spec,
        out_shape=jax.ShapeDtypeStruct((n_rows, D_MODEL), F32),
        compiler_params=pltpu.CompilerParams(dimension_semantics=("arbitrary",),
                                             vmem_limit_bytes=VMEM_LIMIT_BYTES),
        name="moe_experts",
    )(blk_e, src, src, dst, dst, gate, h2, wg, wu, wd)


def _route(rt, n):
    bm = MOE_BLOCK
    top_idx = rt[:, 0:TOP_K].astype(jnp.int32)
    gates = rt[:, TOP_K:2 * TOP_K]
    n_assign = n * TOP_K
    n_rows = n_assign + N_EXPERTS * bm
    nblocks = n_rows // bm
    flat_e = top_idx.reshape(-1)
    order = jnp.argsort(flat_e)
    sorted_e = flat_e[order]
    sizes = jnp.bincount(flat_e, length=N_EXPERTS)
    padded = (sizes + bm - 1) // bm * bm
    group_start = jnp.cumsum(sizes) - sizes
    padded_end = jnp.cumsum(padded)
    padded_start = padded_end - padded
    rank = jnp.arange(n_assign, dtype=jnp.int32) - group_start[sorted_e]
    slot = (padded_start[sorted_e] + rank).astype(jnp.int32)
    assign = jnp.full((n_rows,), -1, jnp.int32).at[slot].set(order.astype(jnp.int32))
    is_pad = assign < 0
    a = jnp.maximum(assign, 0)
    tok = a // TOP_K
    k = a % TOP_K
    src = jnp.where(is_pad, 0, tok)
    pad_rank = jnp.cumsum(is_pad.astype(jnp.int32)) - 1
    dst = jnp.where(is_pad, n_assign + pad_rank, k * n + tok)
    gate = jnp.where(is_pad, 0.0, gates.reshape(-1)[a])
    blk_e = jnp.minimum(jnp.searchsorted(padded_end, jnp.arange(nblocks) * bm, side='right'),
                        N_EXPERTS - 1).astype(jnp.int32)
    spare = n_rows + jnp.arange(bm, dtype=jnp.int32)
    dst = jnp.concatenate([spare, dst]).reshape(nblocks + 1, 1, bm)
    return blk_e, src.reshape(nblocks, 1, bm), dst, gate.reshape(nblocks, bm, 1)


def _final_kernel(x_ref, ya_ref, yb_ref, w_ref, o_ref):
    o_ref[...] = _rms(x_ref[...] + ya_ref[...] + yb_ref[...], w_ref[...])


def _final(x, ymoe, w):
    n = x.shape[0]
    tm = min(1024, n)
    off = n // tm
    tile = pl.BlockSpec((tm, D_MODEL), lambda i: (i, 0))
    return pl.pallas_call(
        _final_kernel,
        grid=(n // tm,),
        in_specs=[tile, tile, pl.BlockSpec((tm, D_MODEL), lambda i: (off + i, 0)), _resident((1, D_MODEL))],
        out_specs=tile,
        out_shape=jax.ShapeDtypeStruct((n, D_MODEL), F32),
        compiler_params=pltpu.CompilerParams(dimension_semantics=("arbitrary",)),
        name="final_norm",
    )(x, ymoe, ymoe, w)


def kernel(x, positions, attn_norm_w, w_in, attn_sinks, conv_w, head_norm_w, w_out, ffn_norm_w, dense_w_gate,
           dense_w_up, dense_w_down, router_w, moe_w_gate, moe_w_up, moe_w_down, final_norm_w):
    bsz, seq, _ = x.shape
    n = bsz * seq
    depth = w_in.shape[0]

    inv_freq = ROPE_THETA ** (-jnp.arange(0, HEAD_DIM, 2, dtype=F32) / HEAD_DIM)
    ang = positions.astype(F32)[..., None] * inv_freq
    cos, sin = jnp.cos(ang), jnp.sin(ang)
    cosf = jnp.tile(cos, (1, 1, 4)).reshape(n, LANES)
    sinf = jnp.tile(jnp.concatenate([-sin, sin], axis=-1), (1, 1, 2)).reshape(n, LANES)

    grp = jnp.arange(MXU_DIM) // HEAD_DIM
    gsum = jnp.where(grp[:, None] == grp[None, :], 1.0 / HEAD_DIM, 0.0).astype(BF16)

    xf = x.reshape(n, D_MODEL)
    ymoe = None
    for l in range(depth):
        is_moe = l % 2 == 1
        i = l // 2
        rw = None
        if is_moe:
            rw = jnp.pad(router_w[i], ((0, 0), (0, LANES - N_EXPERTS))).astype(BF16)
        outs = _mixer(xf, ymoe, cosf, sinf, attn_norm_w[l][None], w_in[l].astype(BF16), attn_sinks[l],
                      conv_w[l], head_norm_w[l][None], w_out[l].astype(BF16), ffn_norm_w[l][None], gsum, rw,
                      seq, F32 if is_moe else BF16)
        if is_moe:
            xf, h2, rt = outs
            blk_e, src, dst, gate = _route(rt, n)
            ymoe = _moe_experts(h2, blk_e, src, dst, gate, moe_w_gate[i].astype(BF16),
                                moe_w_up[i].astype(BF16), moe_w_down[i].astype(BF16))
        else:
            xf, h2 = outs
            xf = _dense_ffn(xf, h2, dense_w_gate[i].astype(BF16), dense_w_up[i].astype(BF16),
                            dense_w_down[i].astype(BF16))
            ymoe = None
    if ymoe is None:
        ymoe = jnp.zeros((2 * n, D_MODEL), F32)
    out = _final(xf, ymoe, final_norm_w[None])
    return out.reshape(bsz, seq, D_MODEL)
```

```python
import functools

import jax
import jax.numpy as jnp
from jax import lax
from jax.experimental import pallas as pl
from jax.experimental.pallas import tpu as pltpu

D_MODEL = 1024
HEAD_DIM = 64
N_Q_HEADS = 8
N_KV_HEADS = 2
ATTN_WIDTH = N_Q_HEADS * HEAD_DIM
KV_WIDTH = N_KV_HEADS * HEAD_DIM
CONV_WIDTH = D_MODEL - ATTN_WIDTH
Q_END = ATTN_WIDTH
K_END = Q_END + KV_WIDTH
V_END = K_END + KV_WIDTH
CB_END = V_END + CONV_WIDTH
CC_END = CB_END + CONV_WIDTH
IN_WIDTH = CC_END + CONV_WIDTH
WINDOW = 128
ROPE_THETA = 10000.0
CONV_K = 3
D_FF = 3584
N_EXPERTS = 8
TOP_K = 2
NORM_EPS = 1e-5
NEG_INF = -1e30

LANES = 128
SUBLANES = 8
MXU_DIM = 256
VMEM_LIMIT_BYTES = 58 * 1024 * 1024

SEQ_TILE = 512
FFN_TILE = 512
FF_CHUNK = 512
MOE_BLOCK = 512

F32 = jnp.float32
BF16 = jnp.bfloat16


def _rms(x, w):
    var = jnp.mean(x * x, axis=-1, keepdims=True)
    return x * lax.rsqrt(var + NORM_EPS) * w


def _dot(a, b):
    return jnp.dot(a, b, preferred_element_type=F32)


def _dot_nt(a, b):
    return lax.dot_general(a, b, (((1,), (1,)), ((), ())), preferred_element_type=F32)


def _moe_combine(x, ya_ref, yb_ref, rt_ref):
    rt = rt_ref[...]
    return x + rt[:, TOP_K:TOP_K + 1] * ya_ref[...] + rt[:, TOP_K + 1:TOP_K + 2] * yb_ref[...]


def _mixer_kernel(ts, moe_in, router, *refs):
    refs = list(refs)
    x_ref = refs.pop(0)
    ya_ref = refs.pop(0) if moe_in else None
    yb_ref = refs.pop(0) if moe_in else None
    rtp_ref = refs.pop(0) if moe_in else None
    (cos_ref, sin_ref, anw_ref, win_ref, sinks_ref, convw_ref, hnw_ref, wout_ref, fnw_ref,
     gsum_ref) = refs[:10]
    refs = refs[10:]
    rw_ref = refs.pop(0) if router else None
    xo_ref = refs.pop(0)
    h2_ref = refs.pop(0)
    rt_ref = refs.pop(0) if router else None
    q_s, k_s, v_s, u_s, y_s = refs

    t = pl.program_id(1)
    nblk = ts // WINDOW

    @pl.when(t == 0)
    def _():
        k_s[:, 0:WINDOW, :] = jnp.zeros((4, WINDOW, LANES), BF16)
        v_s[:, 0:WINDOW, :] = jnp.zeros((4, WINDOW, LANES), BF16)
        u_s[0:SUBLANES, :] = jnp.zeros((SUBLANES, CONV_WIDTH), F32)

    @pl.when(t > 0)
    def _():
        k_s[:, 0:WINDOW, :] = k_s[:, ts:ts + WINDOW, :]
        v_s[:, 0:WINDOW, :] = v_s[:, ts:ts + WINDOW, :]
        u_s[0:SUBLANES, :] = u_s[ts:ts + SUBLANES, :]

    x = x_ref[...]
    if moe_in:
        x = _moe_combine(x, ya_ref, yb_ref, rtp_ref)
    h = _rms(x, anw_ref[...]).astype(BF16)
    proj = _dot(h, win_ref[...])

    cosf = cos_ref[...]
    sinf = sin_ref[...]
    lane = lax.broadcasted_iota(jnp.int32, (ts, LANES), 1)
    first_half = (lane & (HEAD_DIM // 2)) == 0
    low_head = lane < HEAD_DIM

    def rope(tq):
        rot = jnp.where(first_half, pltpu.roll(tq, LANES - HEAD_DIM // 2, 1), pltpu.roll(tq, HEAD_DIM // 2, 1))
        return tq * cosf + rot * sinf

    scale = HEAD_DIM ** -0.5
    for p in range(ATTN_WIDTH // LANES):
        q_s[:, p * LANES:(p + 1) * LANES] = (rope(proj[:, p * LANES:(p + 1) * LANES]) * scale).astype(BF16)

    def lane_variants(a, dst):
        ar = pltpu.roll(a, HEAD_DIM, 1)
        zero = jnp.zeros_like(a)
        dst[0, WINDOW:WINDOW + ts, :] = jnp.where(low_head, a, zero).astype(BF16)
        dst[1, WINDOW:WINDOW + ts, :] = jnp.where(low_head, zero, ar).astype(BF16)
        dst[2, WINDOW:WINDOW + ts, :] = jnp.where(low_head, ar, zero).astype(BF16)
        dst[3, WINDOW:WINDOW + ts, :] = jnp.where(low_head, zero, a).astype(BF16)

    lane_variants(rope(proj[:, Q_END:K_END]), k_s)
    lane_variants(proj[:, K_END:V_END], v_s)

    qi = lax.broadcasted_iota(jnp.int32, (WINDOW, 2 * WINDOW), 0)
    kj = lax.broadcasted_iota(jnp.int32, (WINDOW, 2 * WINDOW), 1)
    band = (kj > qi) & (kj <= qi + WINDOW)
    band_first = band & (kj >= jnp.where(t > 0, 0, WINDOW))

    for b in range(nblk):
        mask = band_first if b == 0 else band
        rows = slice(b * WINDOW, (b + 1) * WINDOW)
        ctx = slice(b * WINDOW, (b + 2) * WINDOW)
        for g in range(N_KV_HEADS):
            qg = jnp.concatenate([q_s[rows, (2 * g) * LANES:(2 * g + 1) * LANES],
                                  q_s[rows, (2 * g + 1) * LANES:(2 * g + 2) * LANES]], axis=0)
            probs = []
            for j in range(2):
                s = _dot_nt(qg, k_s[2 * g + j, ctx, :])
                halves = []
                for r in range(2):
                    sink = sinks_ref[4 * g + 2 * r + j]
                    sr = jnp.where(mask, s[r * WINDOW:(r + 1) * WINDOW], NEG_INF)
                    m = jnp.maximum(jnp.max(sr, axis=-1, keepdims=True), sink)
                    e = jnp.exp(sr - m)
                    denom = jnp.sum(e, axis=-1, keepdims=True) + jnp.exp(sink - m)
                    halves.append((e * (1.0 / denom)).astype(BF16))
                probs.append(jnp.concatenate(halves, axis=0))
            o = _dot(probs[0], v_s[2 * g, ctx, :]) + _dot(probs[1], v_s[2 * g + 1, ctx, :])
            y_s[rows, (2 * g) * LANES:(2 * g + 1) * LANES] = o[0:WINDOW]
            y_s[rows, (2 * g + 1) * LANES:(2 * g + 2) * LANES] = o[WINDOW:2 * WINDOW]

    u_s[SUBLANES:SUBLANES + ts, :] = proj[:, CB_END:CC_END] * proj[:, CC_END:IN_WIDTH]
    cw = convw_ref[...]
    conv = cw[0:1, :] * u_s[SUBLANES - 2:SUBLANES - 2 + ts, :]
    conv = conv + cw[1:2, :] * u_s[SUBLANES - 1:SUBLANES - 1 + ts, :]
    conv = conv + cw[2:3, :] * u_s[SUBLANES:SUBLANES + ts, :]
    y_s[:, ATTN_WIDTH:D_MODEL] = proj[:, V_END:CB_END] * conv

    y = y_s[...]
    sq = (y * y).astype(BF16)
    gsum = gsum_ref[...]
    ms = jnp.concatenate([_dot(sq[:, c * MXU_DIM:(c + 1) * MXU_DIM], gsum) for c in range(D_MODEL // MXU_DIM)],
                         axis=-1)
    yn = (y * lax.rsqrt(ms + NORM_EPS) * hnw_ref[...]).astype(BF16)

    xn = x + _dot(yn, wout_ref[...])
    xo_ref[...] = xn
    h2 = _rms(xn, fnw_ref[...])
    h2_ref[...] = h2.astype(h2_ref.dtype)

    if router:
        lg = _dot(h2.astype(BF16), rw_ref[...])
        lanef = lane.astype(F32)
        l1 = jnp.where(lane < N_EXPERTS, lg, -jnp.inf)
        m1 = jnp.max(l1, axis=-1, keepdims=True)
        i1 = jnp.min(jnp.where(l1 == m1, lanef, float(LANES)), axis=-1, keepdims=True)
        l2 = jnp.where(lanef == i1, -jnp.inf, l1)
        m2 = jnp.max(l2, axis=-1, keepdims=True)
        i2 = jnp.min(jnp.where(l2 == m2, lanef, float(LANES)), axis=-1, keepdims=True)
        e2 = jnp.exp(m2 - m1)
        inv = 1.0 / (1.0 + e2)
        out = jnp.where(lane == 0, i1, jnp.where(lane == 1, i2, jnp.where(lane == 2, inv, e2 * inv)))
        rt_ref[...] = jnp.where(lane < 4, out, 0.0)


def _resident(shape, layer=None):
    if layer is None:
        return pl.BlockSpec(shape, lambda *_: (0,) * len(shape), pipeline_mode=pl.Buffered(1))
    return pl.BlockSpec((None,) + shape, lambda *_: (layer,) + (0,) * len(shape), pipeline_mode=pl.Buffered(1))


def _mixer(x, moe, cosf, sinf, anw, win, sinks, convw, hnw, wout, fnw, gsum, rw, layer, seq, h2_dtype):
    n = x.shape[0]
    ts = min(SEQ_TILE, seq)
    nt = seq // ts
    nb = n // seq
    moe_in = moe is not None
    router = rw is not None
    row = lambda b, t: (b * nt + t, 0)
    tile = pl.BlockSpec((ts, D_MODEL), row)
    lanes_tile = pl.BlockSpec((ts, LANES), row)
    in_specs = [tile]
    args = [x]
    if moe_in:
        ymoe, rt_prev = moe
        off = n // ts
        in_specs += [tile, pl.BlockSpec((ts, D_MODEL), lambda b, t: (off + b * nt + t, 0)), lanes_tile]
        args += [ymoe, ymoe, rt_prev]
    in_specs += [lanes_tile, lanes_tile,
                 _resident((1, D_MODEL), layer), _resident((D_MODEL, IN_WIDTH), layer),
                 pl.BlockSpec(memory_space=pltpu.SMEM),
                 _resident((CONV_K, CONV_WIDTH), layer), _resident((1, D_MODEL), layer),
                 _resident((D_MODEL, D_MODEL), layer), _resident((1, D_MODEL), layer),
                 _resident((MXU_DIM, MXU_DIM))]
    args += [cosf, sinf, anw, win, sinks, convw, hnw, wout, fnw, gsum]
    out_shape = [jax.ShapeDtypeStruct((n, D_MODEL), F32), jax.ShapeDtypeStruct((n, D_MODEL), h2_dtype)]
    out_specs = [tile, tile]
    if router:
        in_specs.append(_resident((D_MODEL, LANES)))
        args.append(rw)
        out_shape.append(jax.ShapeDtypeStruct((n, LANES), F32))
        out_specs.append(lanes_tile)
    return pl.pallas_call(
        functools.partial(_mixer_kernel, ts, moe_in, router),
        grid=(nb, nt),
        in_specs=in_specs,
        out_specs=out_specs,
        out_shape=out_shape,
        scratch_shapes=[pltpu.VMEM((ts, ATTN_WIDTH), BF16),
                        pltpu.VMEM((4, WINDOW + ts, LANES), BF16),
                        pltpu.VMEM((4, WINDOW + ts, LANES), BF16),
                        pltpu.VMEM((SUBLANES + ts, CONV_WIDTH), F32),
                        pltpu.VMEM((ts, D_MODEL), F32)],
        compiler_params=pltpu.CompilerParams(dimension_semantics=("arbitrary", "arbitrary"),
                                             vmem_limit_bytes=VMEM_LIMIT_BYTES),
        name="mixer",
    )(*args)


def _swiglu(hb, wg_ref, wu_ref, wd_ref, acc_ref, side_work=None):
    for c in range(D_FF // FF_CHUNK):
        if side_work is not None:
            side_work(c)
        cols = slice(c * FF_CHUNK, (c + 1) * FF_CHUNK)
        g = _dot(hb, wg_ref[:, cols])
        u = _dot(hb, wu_ref[:, cols])
        a = (jax.nn.silu(g) * u).astype(BF16)
        part = _dot(a, wd_ref[cols, :])
        if c == 0:
            acc_ref[...] = part
        elif c < D_FF // FF_CHUNK - 1:
            acc_ref[...] += part
    return acc_ref[...] + part


def _ffn_kernel(x_ref, h_ref, wg_ref, wu_ref, wd_ref, o_ref, acc_ref):
    o_ref[...] = x_ref[...] + _swiglu(h_ref[...], wg_ref, wu_ref, wd_ref, acc_ref)


def _dense_ffn(x, h2, wg, wu, wd, layer):
    n = x.shape[0]
    tm = min(FFN_TILE, n)
    tile = pl.BlockSpec((tm, D_MODEL), lambda i: (i, 0))
    return pl.pallas_call(
        _ffn_kernel,
        grid=(n // tm,),
        in_specs=[tile, tile, _resident((D_MODEL, D_FF), layer), _resident((D_MODEL, D_FF), layer),
                  _resident((D_FF, D_MODEL), layer)],
        out_specs=tile,
        out_shape=jax.ShapeDtypeStruct((n, D_MODEL), F32),
        scratch_shapes=[pltpu.VMEM((tm, D_MODEL), F32)],
        compiler_params=pltpu.CompilerParams(dimension_semantics=("arbitrary",),
                                             vmem_limit_bytes=VMEM_LIMIT_BYTES),
        name="dense_ffn",
    )(x, h2, wg, wu, wd)


def _moe_kernel(blk_e_ref, src0_ref, srcn_ref, dstp_ref, dstl_ref, h_hbm, wg_ref, wu_ref, wd_ref, y_hbm,
                xbuf, ybuf, acc_ref, gsem, ssem):
    del blk_e_ref
    i = pl.program_id(0)
    last = pl.num_programs(0) - 1
    slot = i % 2
    other = 1 - slot
    bm = MOE_BLOCK
    nchunks = D_FF // FF_CHUNK
    per_chunk = -(-bm // nchunks)

    def gather_row(idx_ref, r, block_slot):
        pltpu.make_async_copy(h_hbm.at[pl.ds(idx_ref[0, 0, r], 1)], xbuf.at[block_slot, pl.ds(r, 1)],
                              gsem.at[block_slot]).start()

    def scatter_row(idx_ref, r, block_slot):
        pltpu.make_async_copy(ybuf.at[block_slot, pl.ds(r, 1)], y_hbm.at[pl.ds(idx_ref[0, 0, r], 1)],
                              ssem.at[block_slot]).start()

    def wait_gather(block_slot):
        pltpu.make_async_copy(h_hbm.at[pl.ds(0, bm)], xbuf.at[block_slot], gsem.at[block_slot]).wait()

    def wait_scatter(block_slot):
        pltpu.make_async_copy(ybuf.at[block_slot], y_hbm.at[pl.ds(0, bm)], ssem.at[block_slot]).wait()

    @pl.when(i == 0)
    def _():
        def body(r, c):
            gather_row(src0_ref, r, 0)
            return c
        lax.fori_loop(0, bm, body, 0)
        ybuf[1] = jnp.zeros((bm, D_MODEL), F32)

    @pl.when(i >= 1)
    def _():
        wait_scatter(slot)

    wait_gather(slot)

    def side_work(c):
        for r in range(c * per_chunk, min((c + 1) * per_chunk, bm)):
            gather_row(srcn_ref, r, other)
            scatter_row(dstp_ref, r, other)

    ybuf[slot] = _swiglu(xbuf[slot].astype(BF16), wg_ref, wu_ref, wd_ref, acc_ref, side_work)

    @pl.when(i == last)
    def _():
        def body(r, c):
            scatter_row(dstl_ref, r, slot)
            return c
        lax.fori_loop(0, bm, body, 0)
        wait_scatter(slot)
        wait_scatter(other)
        wait_gather(other)


def _moe_experts(h2, blk_e, src, dst, wg, wu, wd, layer):
    nblocks = src.shape[0]
    bm = MOE_BLOCK
    n_rows = (nblocks + 1) * bm
    wspec_in = pl.BlockSpec((None, None, D_MODEL, D_FF), lambda i, be: (layer, be[i], 0, 0),
                            pipeline_mode=pl.Buffered(1))
    wspec_out = pl.BlockSpec((None, None, D_FF, D_MODEL), lambda i, be: (layer, be[i], 0, 0),
                             pipeline_mode=pl.Buffered(1))
    smem_rows = lambda index_map: pl.BlockSpec((1, 1, bm), index_map, memory_space=pltpu.SMEM)
    grid_spec = pltpu.PrefetchScalarGridSpec(
        num_scalar_prefetch=1,
        grid=(nblocks,),
        in_specs=[smem_rows(lambda i, be: (0, 0, 0)),
                  smem_rows(lambda i, be: (jnp.minimum(i + 1, nblocks - 1), 0, 0)),
                  smem_rows(lambda i, be: (i, 0, 0)),
                  smem_rows(lambda i, be: (nblocks, 0, 0)),
                  pl.BlockSpec(memory_space=pl.ANY),
                  wspec_in, wspec_in, wspec_out],
        out_specs=pl.BlockSpec(memory_space=pl.ANY),
        scratch_shapes=[pltpu.VMEM((2, bm, D_MODEL), F32),
                        pltpu.VMEM((2, bm, D_MODEL), F32),
                        pltpu.VMEM((bm, D_MODEL), F32),
                        pltpu.SemaphoreType.DMA((2,)),
                        pltpu.SemaphoreType.DMA((2,))],
    )
    return pl.pallas_call(
        _moe_kernel,
        grid_spec=grid_spec,
        out_shape=jax.ShapeDtypeStruct((n_rows, D_MODEL), F32),
        compiler_params=pltpu.CompilerParams(dimension_semantics=("arbitrary",),
                                             vmem_limit_bytes=VMEM_LIMIT_BYTES),
        name="moe_experts",
    )(blk_e, src, src, dst, dst, h2, wg, wu, wd)


def _route(rt, n):
    bm = MOE_BLOCK
    n_assign = n * TOP_K
    n_rows = n_assign + N_EXPERTS * bm
    nblocks = n_rows // bm
    flat_e = rt[:, 0:TOP_K].astype(jnp.int32).reshape(-1)
    order = jnp.argsort(flat_e).astype(jnp.int32)
    experts = jnp.arange(N_EXPERTS, dtype=jnp.int32)
    sizes = jnp.sum((flat_e[:, None] == experts[None, :]).astype(jnp.int32), axis=0)
    padded = (sizes + bm - 1) // bm * bm
    group_end = jnp.cumsum(sizes)
    padded_end = jnp.cumsum(padded)
    blk_start = jnp.arange(nblocks, dtype=jnp.int32) * bm
    blk_e = jnp.minimum(jnp.sum((blk_start[:, None] >= padded_end[None, :]).astype(jnp.int32), axis=1),
                        N_EXPERTS - 1)
    rank = blk_start[:, None] + jnp.arange(bm, dtype=jnp.int32)[None, :] - (padded_end - padded)[blk_e][:, None]
    valid = rank < sizes[blk_e][:, None]
    sorted_pos = (group_end - sizes)[blk_e][:, None] + rank
    assign = order[jnp.clip(sorted_pos, 0, n_assign - 1)]
    tok = assign // TOP_K
    src = jnp.where(valid, tok, 0)
    slot = blk_start[:, None] + jnp.arange(bm, dtype=jnp.int32)[None, :]
    dst = jnp.where(valid, (assign % TOP_K) * n + tok, n_assign + slot - group_end[blk_e][:, None])
    spare = n_rows + jnp.arange(bm, dtype=jnp.int32)
    dst = jnp.concatenate([spare[None, :], dst], axis=0)
    return blk_e, src.reshape(nblocks, 1, bm), dst.reshape(nblocks + 1, 1, bm)


def _final_kernel(moe_in, *refs):
    if moe_in:
        x_ref, ya_ref, yb_ref, rt_ref, w_ref, o_ref = refs
        x = _moe_combine(x_ref[...], ya_ref, yb_ref, rt_ref)
    else:
        x_ref, w_ref, o_ref = refs
        x = x_ref[...]
    o_ref[...] = _rms(x, w_ref[...])


def _final(x, moe, w):
    n = x.shape[0]
    tm = min(1024, n)
    off = n // tm
    tile = pl.BlockSpec((tm, D_MODEL), lambda i: (i, 0))
    in_specs = [tile]
    args = [x]
    if moe is not None:
        in_specs += [tile, pl.BlockSpec((tm, D_MODEL), lambda i: (off + i, 0)),
                     pl.BlockSpec((tm, LANES), lambda i: (i, 0))]
        args += [moe[0], moe[0], moe[1]]
    return pl.pallas_call(
        functools.partial(_final_kernel, moe is not None),
        grid=(n // tm,),
        in_specs=in_specs + [_resident((1, D_MODEL))],
        out_specs=tile,
        out_shape=jax.ShapeDtypeStruct((n, D_MODEL), F32),
        compiler_params=pltpu.CompilerParams(dimension_semantics=("arbitrary",)),
        name="final_norm",
    )(*args, w)


def kernel(x, positions, attn_norm_w, w_in, attn_sinks, conv_w, head_norm_w, w_out, ffn_norm_w, dense_w_gate,
           dense_w_up, dense_w_down, router_w, moe_w_gate, moe_w_up, moe_w_down, final_norm_w):
    bsz, seq, _ = x.shape
    n = bsz * seq
    depth = w_in.shape[0]

    inv_freq = ROPE_THETA ** (-jnp.arange(0, HEAD_DIM, 2, dtype=F32) / HEAD_DIM)
    ang = positions.astype(F32)[..., None] * inv_freq
    cos, sin = jnp.cos(ang), jnp.sin(ang)
    cosf = jnp.tile(cos, (1, 1, 4)).reshape(n, LANES)
    sinf = jnp.tile(jnp.concatenate([-sin, sin], axis=-1), (1, 1, 2)).reshape(n, LANES)

    grp = jnp.arange(MXU_DIM) // HEAD_DIM
    gsum = jnp.where(grp[:, None] == grp[None, :], 1.0 / HEAD_DIM, 0.0).astype(BF16)

    win_b, wout_b = w_in.astype(BF16), w_out.astype(BF16)
    dense_b = [w.astype(BF16) for w in (dense_w_gate, dense_w_up, dense_w_down)]
    moe_b = [w.astype(BF16) for w in (moe_w_gate, moe_w_up, moe_w_down)]
    rw_b = jnp.pad(router_w, ((0, 0), (0, 0), (0, LANES - N_EXPERTS))).astype(BF16)
    anw, hnw, fnw = (w[:, None, :] for w in (attn_norm_w, head_norm_w, ffn_norm_w))

    xf = x.reshape(n, D_MODEL)
    moe = None
    for l in range(depth):
        is_moe = l % 2 == 1
        i = l // 2
        outs = _mixer(xf, moe, cosf, sinf, anw, win_b, attn_sinks[l], conv_w, hnw, wout_b, fnw, gsum,
                      rw_b[i] if is_moe else None, l, seq, F32 if is_moe else BF16)
        if is_moe:
            xf, h2, rt = outs
            blk_e, src, dst = _route(rt, n)
            moe = (_moe_experts(h2, blk_e, src, dst, *moe_b, i), rt)
        else:
            xf, h2 = outs
            xf = _dense_ffn(xf, h2, *dense_b, i)
            moe = None
    out = _final(xf, moe, final_norm_w[None])
    return out.reshape(bsz, seq, D_MODEL)
```

```python
import functools

import jax
import jax.numpy as jnp
from jax import lax
from jax.experimental import pallas as pl
from jax.experimental.pallas import tpu as pltpu

D_MODEL = 1024
HEAD_DIM = 64
N_Q_HEADS = 8
N_KV_HEADS = 2
ATTN_WIDTH = N_Q_HEADS * HEAD_DIM
KV_WIDTH = N_KV_HEADS * HEAD_DIM
CONV_WIDTH = D_MODEL - ATTN_WIDTH
Q_END = ATTN_WIDTH
K_END = Q_END + KV_WIDTH
V_END = K_END + KV_WIDTH
CB_END = V_END + CONV_WIDTH
CC_END = CB_END + CONV_WIDTH
IN_WIDTH = CC_END + CONV_WIDTH
WINDOW = 128
ROPE_THETA = 10000.0
CONV_K = 3
D_FF = 3584
N_EXPERTS = 8
TOP_K = 2
NORM_EPS = 1e-5
NEG_INF = -1e30

LANES = 128
SUBLANES = 8
MXU_DIM = 256
VMEM_LIMIT_BYTES = 58 * 1024 * 1024

SEQ_TILE = 512
FFN_TILE = 512
FF_CHUNK = 512
MOE_BLOCK = 512
MOE_ISSUE_CHUNKS = 4

F32 = jnp.float32
BF16 = jnp.bfloat16


def _rms(x, w):
    var = jnp.mean(x * x, axis=-1, keepdims=True)
    return x * lax.rsqrt(var + NORM_EPS) * w


def _dot(a, b):
    return jnp.dot(a, b, preferred_element_type=F32)


def _dot_nt(a, b):
    return lax.dot_general(a, b, (((1,), (1,)), ((), ())), preferred_element_type=F32)


def _moe_combine(x, ya_ref, yb_ref, rt_ref):
    rt = rt_ref[...]
    return x + rt[:, TOP_K:TOP_K + 1] * ya_ref[...] + rt[:, TOP_K + 1:TOP_K + 2] * yb_ref[...]


def _mixer_kernel(ts, moe_in, router, *refs):
    refs = list(refs)
    x_ref = refs.pop(0)
    ya_ref = refs.pop(0) if moe_in else None
    yb_ref = refs.pop(0) if moe_in else None
    rtp_ref = refs.pop(0) if moe_in else None
    (cos_ref, sin_ref, anw_ref, win_ref, sinks_ref, convw_ref, hnw_ref, wout_ref, fnw_ref,
     gsum_ref) = refs[:10]
    refs = refs[10:]
    rw_ref = refs.pop(0) if router else None
    xo_ref = refs.pop(0)
    h2_ref = refs.pop(0)
    rt_ref = refs.pop(0) if router else None
    q_s, k_s, v_s, u_s, y_s = refs

    t = pl.program_id(1)
    nblk = ts // WINDOW

    @pl.when(t == 0)
    def _():
        k_s[:, 0:WINDOW, :] = jnp.zeros((4, WINDOW, LANES), BF16)
        v_s[:, 0:WINDOW, :] = jnp.zeros((4, WINDOW, LANES), BF16)
        u_s[0:SUBLANES, :] = jnp.zeros((SUBLANES, CONV_WIDTH), F32)

    @pl.when(t > 0)
    def _():
        k_s[:, 0:WINDOW, :] = k_s[:, ts:ts + WINDOW, :]
        v_s[:, 0:WINDOW, :] = v_s[:, ts:ts + WINDOW, :]
        u_s[0:SUBLANES, :] = u_s[ts:ts + SUBLANES, :]

    x = x_ref[...]
    if moe_in:
        x = _moe_combine(x, ya_ref, yb_ref, rtp_ref)
    h = _rms(x, anw_ref[...]).astype(BF16)
    proj = _dot(h, win_ref[...])

    cosf = cos_ref[...]
    sinf = sin_ref[...]
    lane = lax.broadcasted_iota(jnp.int32, (ts, LANES), 1)
    first_half = (lane & (HEAD_DIM // 2)) == 0
    low_head = lane < HEAD_DIM

    def rope(tq):
        rot = jnp.where(first_half, pltpu.roll(tq, LANES - HEAD_DIM // 2, 1), pltpu.roll(tq, HEAD_DIM // 2, 1))
        return tq * cosf + rot * sinf

    scale = HEAD_DIM ** -0.5
    for p in range(ATTN_WIDTH // LANES):
        q_s[:, p * LANES:(p + 1) * LANES] = (rope(proj[:, p * LANES:(p + 1) * LANES]) * scale).astype(BF16)

    def lane_variants(a, dst):
        ar = pltpu.roll(a, HEAD_DIM, 1)
        zero = jnp.zeros_like(a)
        dst[0, WINDOW:WINDOW + ts, :] = jnp.where(low_head, a, zero).astype(BF16)
        dst[1, WINDOW:WINDOW + ts, :] = jnp.where(low_head, zero, ar).astype(BF16)
        dst[2, WINDOW:WINDOW + ts, :] = jnp.where(low_head, ar, zero).astype(BF16)
        dst[3, WINDOW:WINDOW + ts, :] = jnp.where(low_head, zero, a).astype(BF16)

    lane_variants(rope(proj[:, Q_END:K_END]), k_s)
    lane_variants(proj[:, K_END:V_END], v_s)

    qi = lax.broadcasted_iota(jnp.int32, (WINDOW, 2 * WINDOW), 0)
    kj = lax.broadcasted_iota(jnp.int32, (WINDOW, 2 * WINDOW), 1)
    band = (kj > qi) & (kj <= qi + WINDOW)
    band_first = band & (kj >= jnp.where(t > 0, 0, WINDOW))

    for b in range(nblk):
        mask = band_first if b == 0 else band
        rows = slice(b * WINDOW, (b + 1) * WINDOW)
        ctx = slice(b * WINDOW, (b + 2) * WINDOW)
        for g in range(N_KV_HEADS):
            qg = jnp.concatenate([q_s[rows, (2 * g) * LANES:(2 * g + 1) * LANES],
                                  q_s[rows, (2 * g + 1) * LANES:(2 * g + 2) * LANES]], axis=0)
            probs = []
            for j in range(2):
                s = _dot_nt(qg, k_s[2 * g + j, ctx, :])
                halves = []
                for r in range(2):
                    sink = sinks_ref[4 * g + 2 * r + j]
                    sr = jnp.where(mask, s[r * WINDOW:(r + 1) * WINDOW], NEG_INF)
                    m = jnp.maximum(jnp.max(sr, axis=-1, keepdims=True), sink)
                    e = jnp.exp(sr - m)
                    denom = jnp.sum(e, axis=-1, keepdims=True) + jnp.exp(sink - m)
                    halves.append((e * (1.0 / denom)).astype(BF16))
                probs.append(jnp.concatenate(halves, axis=0))
            o = _dot(probs[0], v_s[2 * g, ctx, :]) + _dot(probs[1], v_s[2 * g + 1, ctx, :])
            y_s[rows, (2 * g) * LANES:(2 * g + 1) * LANES] = o[0:WINDOW]
            y_s[rows, (2 * g + 1) * LANES:(2 * g + 2) * LANES] = o[WINDOW:2 * WINDOW]

    u_s[SUBLANES:SUBLANES + ts, :] = proj[:, CB_END:CC_END] * proj[:, CC_END:IN_WIDTH]
    cw = convw_ref[...]
    conv = cw[0:1, :] * u_s[SUBLANES - 2:SUBLANES - 2 + ts, :]
    conv = conv + cw[1:2, :] * u_s[SUBLANES - 1:SUBLANES - 1 + ts, :]
    conv = conv + cw[2:3, :] * u_s[SUBLANES:SUBLANES + ts, :]
    y_s[:, ATTN_WIDTH:D_MODEL] = proj[:, V_END:CB_END] * conv

    y = y_s[...]
    sq = (y * y).astype(BF16)
    gsum = gsum_ref[...]
    ms = jnp.concatenate([_dot(sq[:, c * MXU_DIM:(c + 1) * MXU_DIM], gsum) for c in range(D_MODEL // MXU_DIM)],
                         axis=-1)
    yn = (y * lax.rsqrt(ms + NORM_EPS) * hnw_ref[...]).astype(BF16)

    xn = x + _dot(yn, wout_ref[...])
    xo_ref[...] = xn
    h2 = _rms(xn, fnw_ref[...])
    h2_ref[...] = h2.astype(h2_ref.dtype)

    if router:
        lg = _dot(h2.astype(BF16), rw_ref[...])
        lanef = lane.astype(F32)
        l1 = jnp.where(lane < N_EXPERTS, lg, -jnp.inf)
        m1 = jnp.max(l1, axis=-1, keepdims=True)
        i1 = jnp.min(jnp.where(l1 == m1, lanef, float(LANES)), axis=-1, keepdims=True)
        l2 = jnp.where(lanef == i1, -jnp.inf, l1)
        m2 = jnp.max(l2, axis=-1, keepdims=True)
        i2 = jnp.min(jnp.where(l2 == m2, lanef, float(LANES)), axis=-1, keepdims=True)
        e2 = jnp.exp(m2 - m1)
        inv = 1.0 / (1.0 + e2)
        out = jnp.where(lane == 0, i1, jnp.where(lane == 1, i2, jnp.where(lane == 2, inv, e2 * inv)))
        rt_ref[...] = jnp.where(lane < 4, out, 0.0)


def _resident(shape, layer=None):
    if layer is None:
        return pl.BlockSpec(shape, lambda *_: (0,) * len(shape), pipeline_mode=pl.Buffered(1))
    return pl.BlockSpec((None,) + shape, lambda *_: (layer,) + (0,) * len(shape), pipeline_mode=pl.Buffered(1))


def _mixer(x, moe, cosf, sinf, anw, win, sinks, convw, hnw, wout, fnw, gsum, rw, layer, seq, h2_dtype):
    n = x.shape[0]
    ts = min(SEQ_TILE, seq)
    nt = seq // ts
    nb = n // seq
    moe_in = moe is not None
    router = rw is not None
    row = lambda b, t: (b * nt + t, 0)
    tile = pl.BlockSpec((ts, D_MODEL), row)
    lanes_tile = pl.BlockSpec((ts, LANES), row)
    in_specs = [tile]
    args = [x]
    if moe_in:
        ymoe, rt_prev = moe
        off = n // ts
        in_specs += [tile, pl.BlockSpec((ts, D_MODEL), lambda b, t: (off + b * nt + t, 0)), lanes_tile]
        args += [ymoe, ymoe, rt_prev]
    in_specs += [lanes_tile, lanes_tile,
                 _resident((1, D_MODEL), layer), _resident((D_MODEL, IN_WIDTH), layer),
                 pl.BlockSpec(memory_space=pltpu.SMEM),
                 _resident((CONV_K, CONV_WIDTH), layer), _resident((1, D_MODEL), layer),
                 _resident((D_MODEL, D_MODEL), layer), _resident((1, D_MODEL), layer),
                 _resident((MXU_DIM, MXU_DIM))]
    args += [cosf, sinf, anw, win, sinks, convw, hnw, wout, fnw, gsum]
    out_shape = [jax.ShapeDtypeStruct((n, D_MODEL), F32), jax.ShapeDtypeStruct((n, D_MODEL), h2_dtype)]
    out_specs = [tile, tile]
    if router:
        in_specs.append(_resident((D_MODEL, LANES)))
        args.append(rw)
        out_shape.append(jax.ShapeDtypeStruct((n, LANES), F32))
        out_specs.append(lanes_tile)
    return pl.pallas_call(
        functools.partial(_mixer_kernel, ts, moe_in, router),
        grid=(nb, nt),
        in_specs=in_specs,
        out_specs=out_specs,
        out_shape=out_shape,
        scratch_shapes=[pltpu.VMEM((ts, ATTN_WIDTH), BF16),
                        pltpu.VMEM((4, WINDOW + ts, LANES), BF16),
                        pltpu.VMEM((4, WINDOW + ts, LANES), BF16),
                        pltpu.VMEM((SUBLANES + ts, CONV_WIDTH), F32),
                        pltpu.VMEM((ts, D_MODEL), F32)],
        compiler_params=pltpu.CompilerParams(dimension_semantics=("arbitrary", "arbitrary"),
                                             vmem_limit_bytes=VMEM_LIMIT_BYTES),
        name="mixer",
    )(*args)


def _swiglu(hb, wg_ref, wu_ref, wd_ref, acc_ref, side_work=None):
    for c in range(D_FF // FF_CHUNK):
        if side_work is not None:
            side_work(c)
        cols = slice(c * FF_CHUNK, (c + 1) * FF_CHUNK)
        g = _dot(hb, wg_ref[:, cols])
        u = _dot(hb, wu_ref[:, cols])
        a = (jax.nn.silu(g) * u).astype(BF16)
        part = _dot(a, wd_ref[cols, :])
        if c == 0:
            acc_ref[...] = part
        elif c < D_FF // FF_CHUNK - 1:
            acc_ref[...] += part
    return acc_ref[...] + part


def _ffn_kernel(x_ref, h_ref, wg_ref, wu_ref, wd_ref, o_ref, acc_ref):
    o_ref[...] = x_ref[...] + _swiglu(h_ref[...], wg_ref, wu_ref, wd_ref, acc_ref)


def _dense_ffn(x, h2, wg, wu, wd, layer):
    n = x.shape[0]
    tm = min(FFN_TILE, n)
    tile = pl.BlockSpec((tm, D_MODEL), lambda i: (i, 0))
    return pl.pallas_call(
        _ffn_kernel,
        grid=(n // tm,),
        in_specs=[tile, tile, _resident((D_MODEL, D_FF), layer), _resident((D_MODEL, D_FF), layer),
                  _resident((D_FF, D_MODEL), layer)],
        out_specs=tile,
        out_shape=jax.ShapeDtypeStruct((n, D_MODEL), F32),
        scratch_shapes=[pltpu.VMEM((tm, D_MODEL), F32)],
        compiler_params=pltpu.CompilerParams(dimension_semantics=("arbitrary",),
                                             vmem_limit_bytes=VMEM_LIMIT_BYTES),
        name="dense_ffn",
    )(x, h2, wg, wu, wd)


def _moe_kernel(blk_e_ref, src0_ref, srcn_ref, dstp_ref, dstl_ref, h_hbm, wg_ref, wu_ref, wd_ref, y_hbm,
                xbuf, ybuf, acc_ref, gsem, ssem):
    del blk_e_ref
    i = pl.program_id(0)
    last = pl.num_programs(0) - 1
    slot = i % 2
    other = 1 - slot
    bm = MOE_BLOCK
    per_chunk = bm // MOE_ISSUE_CHUNKS

    def gather_row(idx_ref, r, block_slot):
        pltpu.make_async_copy(h_hbm.at[pl.ds(idx_ref[0, 0, r], 1)], xbuf.at[block_slot, pl.ds(r, 1)],
                              gsem.at[block_slot]).start(priority=0)

    def scatter_row(idx_ref, r, block_slot):
        pltpu.make_async_copy(ybuf.at[block_slot, pl.ds(r, 1)], y_hbm.at[pl.ds(idx_ref[0, 0, r], 1)],
                              ssem.at[block_slot]).start(priority=1)

    def wait_gather(block_slot):
        pltpu.make_async_copy(h_hbm.at[pl.ds(0, bm)], xbuf.at[block_slot], gsem.at[block_slot]).wait()

    def wait_scatter(block_slot):
        pltpu.make_async_copy(ybuf.at[block_slot], y_hbm.at[pl.ds(0, bm)], ssem.at[block_slot]).wait()

    @pl.when(i == 0)
    def _():
        def body(r, c):
            gather_row(src0_ref, r, 0)
            return c
        lax.fori_loop(0, bm, body, 0)
        ybuf[...] = jnp.zeros((2, bm, D_MODEL), F32)
        pltpu.make_async_copy(ybuf.at[0], y_hbm.at[pl.ds(y_hbm.shape[0] - bm, bm)], ssem.at[0]).start()

    wait_gather(slot)

    def side_work(c):
        if c < MOE_ISSUE_CHUNKS:
            for r in range(c * per_chunk, (c + 1) * per_chunk):
                gather_row(srcn_ref, r, other)
                scatter_row(dstp_ref, r, other)

    total = _swiglu(xbuf[slot].astype(BF16), wg_ref, wu_ref, wd_ref, acc_ref, side_work)
    wait_scatter(slot)
    ybuf[slot] = total

    @pl.when(i == last)
    def _():
        def body(r, c):
            scatter_row(dstl_ref, r, slot)
            return c
        lax.fori_loop(0, bm, body, 0)
        wait_scatter(slot)
        wait_scatter(other)
        wait_gather(other)


def _moe_experts(h2, blk_e, src, dst, wg, wu, wd, layer):
    nblocks = src.shape[0]
    bm = MOE_BLOCK
    n_rows = (nblocks + 2) * bm
    wspec_in = pl.BlockSpec((None, None, D_MODEL, D_FF), lambda i, be: (layer, be[i], 0, 0),
                            pipeline_mode=pl.Buffered(1))
    wspec_out = pl.BlockSpec((None, None, D_FF, D_MODEL), lambda i, be: (layer, be[i], 0, 0),
                             pipeline_mode=pl.Buffered(1))
    smem_rows = lambda index_map: pl.BlockSpec((1, 1, bm), index_map, memory_space=pltpu.SMEM)
    grid_spec = pltpu.PrefetchScalarGridSpec(
        num_scalar_prefetch=1,
        grid=(nblocks,),
        in_specs=[smem_rows(lambda i, be: (0, 0, 0)),
                  smem_rows(lambda i, be: (jnp.minimum(i + 1, nblocks - 1), 0, 0)),
                  smem_rows(lambda i, be: (i, 0, 0)),
                  smem_rows(lambda i, be: (nblocks, 0, 0)),
                  pl.BlockSpec(memory_space=pl.ANY),
                  wspec_in, wspec_in, wspec_out],
        out_specs=pl.BlockSpec(memory_space=pl.ANY),
        scratch_shapes=[pltpu.VMEM((2, bm, D_MODEL), F32),
                        pltpu.VMEM((2, bm, D_MODEL), F32),
                        pltpu.VMEM((bm, D_MODEL), F32),
                        pltpu.SemaphoreType.DMA((2,)),
                        pltpu.SemaphoreType.DMA((2,))],
    )
    return pl.pallas_call(
        _moe_kernel,
        grid_spec=grid_spec,
        out_shape=jax.ShapeDtypeStruct((n_rows, D_MODEL), F32),
        compiler_params=pltpu.CompilerParams(dimension_semantics=("arbitrary",),
                                             vmem_limit_bytes=VMEM_LIMIT_BYTES),
        name="moe_experts",
    )(blk_e, src, src, dst, dst, h2, wg, wu, wd)


def _route(rt, n):
    bm = MOE_BLOCK
    n_assign = n * TOP_K
    n_rows = n_assign + N_EXPERTS * bm
    nblocks = n_rows // bm
    flat_e = rt[:, 0:TOP_K].astype(jnp.int32).reshape(-1)
    order = jnp.argsort(flat_e).astype(jnp.int32)
    experts = jnp.arange(N_EXPERTS, dtype=jnp.int32)
    sizes = jnp.sum((flat_e[:, None] == experts[None, :]).astype(jnp.int32), axis=0)
    padded = (sizes + bm - 1) // bm * bm
    group_end = jnp.cumsum(sizes)
    padded_end = jnp.cumsum(padded)
    blk_start = jnp.arange(nblocks, dtype=jnp.int32) * bm
    blk_e = jnp.minimum(jnp.sum((blk_start[:, None] >= padded_end[None, :]).astype(jnp.int32), axis=1),
                        N_EXPERTS - 1)
    rank = blk_start[:, None] + jnp.arange(bm, dtype=jnp.int32)[None, :] - (padded_end - padded)[blk_e][:, None]
    valid = rank < sizes[blk_e][:, None]
    sorted_pos = (group_end - sizes)[blk_e][:, None] + rank
    assign = order[jnp.clip(sorted_pos, 0, n_assign - 1)]
    tok = assign // TOP_K
    src = jnp.where(valid, tok, 0)
    slot = blk_start[:, None] + jnp.arange(bm, dtype=jnp.int32)[None, :]
    dst = jnp.where(valid, (assign % TOP_K) * n + tok, n_assign + slot - group_end[blk_e][:, None])
    spare = n_rows + jnp.arange(bm, dtype=jnp.int32)
    dst = jnp.concatenate([spare[None, :], dst], axis=0)
    return blk_e, src.reshape(nblocks, 1, bm), dst.reshape(nblocks + 1, 1, bm)


def _final_kernel(moe_in, *refs):
    if moe_in:
        x_ref, ya_ref, yb_ref, rt_ref, w_ref, o_ref = refs
        x = _moe_combine(x_ref[...], ya_ref, yb_ref, rt_ref)
    else:
        x_ref, w_ref, o_ref = refs
        x = x_ref[...]
    o_ref[...] = _rms(x, w_ref[...])


def _final(x, moe, w):
    n = x.shape[0]
    tm = min(1024, n)
    off = n // tm
    tile = pl.BlockSpec((tm, D_MODEL), lambda i: (i, 0))
    in_specs = [tile]
    args = [x]
    if moe is not None:
        in_specs += [tile, pl.BlockSpec((tm, D_MODEL), lambda i: (off + i, 0)),
                     pl.BlockSpec((tm, LANES), lambda i: (i, 0))]
        args += [moe[0], moe[0], moe[1]]
    return pl.pallas_call(
        functools.partial(_final_kernel, moe is not None),
        grid=(n // tm,),
        in_specs=in_specs + [_resident((1, D_MODEL))],
        out_specs=tile,
        out_shape=jax.ShapeDtypeStruct((n, D_MODEL), F32),
        compiler_params=pltpu.CompilerParams(dimension_semantics=("arbitrary",)),
        name="final_norm",
    )(*args, w)


def kernel(x, positions, attn_norm_w, w_in, attn_sinks, conv_w, head_norm_w, w_out, ffn_norm_w, dense_w_gate,
           dense_w_up, dense_w_down, router_w, moe_w_gate, moe_w_up, moe_w_down, final_norm_w):
    bsz, seq, _ = x.shape
    n = bsz * seq
    depth = w_in.shape[0]

    inv_freq = ROPE_THETA ** (-jnp.arange(0, HEAD_DIM, 2, dtype=F32) / HEAD_DIM)
    ang = positions.astype(F32)[..., None] * inv_freq
    cos, sin = jnp.cos(ang), jnp.sin(ang)
    cosf = jnp.tile(cos, (1, 1, 4)).reshape(n, LANES)
    sinf = jnp.tile(jnp.concatenate([-sin, sin], axis=-1), (1, 1, 2)).reshape(n, LANES)

    grp = jnp.arange(MXU_DIM) // HEAD_DIM
    gsum = jnp.where(grp[:, None] == grp[None, :], 1.0 / HEAD_DIM, 0.0).astype(BF16)

    win_b, wout_b = w_in.astype(BF16), w_out.astype(BF16)
    dense_b = [w.astype(BF16) for w in (dense_w_gate, dense_w_up, dense_w_down)]
    moe_b = [w.astype(BF16) for w in (moe_w_gate, moe_w_up, moe_w_down)]
    rw_b = jnp.pad(router_w, ((0, 0), (0, 0), (0, LANES - N_EXPERTS))).astype(BF16)
    anw, hnw, fnw = (w[:, None, :] for w in (attn_norm_w, head_norm_w, ffn_norm_w))

    xf = x.reshape(n, D_MODEL)
    moe = None
    for l in range(depth):
        is_moe = l % 2 == 1
        i = l // 2
        outs = _mixer(xf, moe, cosf, sinf, anw, win_b, attn_sinks[l], conv_w, hnw, wout_b, fnw, gsum,
                      rw_b[i] if is_moe else None, l, seq, F32 if is_moe else BF16)
        if is_moe:
            xf, h2, rt = outs
            blk_e, src, dst = _route(rt, n)
            moe = (_moe_experts(h2, blk_e, src, dst, *moe_b, i), rt)
        else:
            xf, h2 = outs
            xf = _dense_ffn(xf, h2, *dense_b, i)
            moe = None
    out = _final(xf, moe, final_norm_w[None])
    return out.reshape(bsz, seq, D_MODEL)
```

```python
import functools

import jax
import jax.numpy as jnp
from jax import lax
from jax.experimental import pallas as pl
from jax.experimental.pallas import tpu as pltpu

D_MODEL = 1024
HEAD_DIM = 64
N_Q_HEADS = 8
N_KV_HEADS = 2
ATTN_WIDTH = N_Q_HEADS * HEAD_DIM
KV_WIDTH = N_KV_HEADS * HEAD_DIM
CONV_WIDTH = D_MODEL - ATTN_WIDTH
Q_END = ATTN_WIDTH
K_END = Q_END + KV_WIDTH
V_END = K_END + KV_WIDTH
CB_END = V_END + CONV_WIDTH
CC_END = CB_END + CONV_WIDTH
IN_WIDTH = CC_END + CONV_WIDTH
WINDOW = 128
ROPE_THETA = 10000.0
CONV_K = 3
D_FF = 3584
N_EXPERTS = 8
TOP_K = 2
NORM_EPS = 1e-5
NEG_INF = -1e30

LANES = 128
SUBLANES = 8
MXU_DIM = 256
VMEM_LIMIT_BYTES = 58 * 1024 * 1024

SEQ_TILE = 512
FFN_TILE = 512
FF_CHUNK = 512
MOE_BLOCK = 512
MOE_ISSUE_CHUNKS = 4

F32 = jnp.float32
BF16 = jnp.bfloat16


def _rms(x, w):
    var = jnp.mean(x * x, axis=-1, keepdims=True)
    return x * lax.rsqrt(var + NORM_EPS) * w


def _dot(a, b):
    return jnp.dot(a, b, preferred_element_type=F32)


def _dot_nt(a, b):
    return lax.dot_general(a, b, (((1,), (1,)), ((), ())), preferred_element_type=F32)


def _moe_combine(x, ya_ref, yb_ref, rt_ref):
    rt = rt_ref[...]
    return x + rt[:, TOP_K:TOP_K + 1] * ya_ref[...] + rt[:, TOP_K + 1:TOP_K + 2] * yb_ref[...]


def _mixer_kernel(ts, moe_in, router, *refs):
    refs = list(refs)
    x_ref = refs.pop(0)
    ya_ref = refs.pop(0) if moe_in else None
    yb_ref = refs.pop(0) if moe_in else None
    rtp_ref = refs.pop(0) if moe_in else None
    (cos_ref, sin_ref, anw_ref, win_ref, sinks_ref, convw_ref, hnw_ref, wout_ref, fnw_ref,
     gsum_ref) = refs[:10]
    refs = refs[10:]
    rw_ref = refs.pop(0) if router else None
    xo_ref = refs.pop(0)
    h2_ref = refs.pop(0)
    rt_ref = refs.pop(0) if router else None
    q_s, k_s, v_s, u_s, y_s, c_s = refs

    t = pl.program_id(1)
    nblk = ts // WINDOW

    @pl.when(t == 0)
    def _():
        k_s[:, 0:WINDOW, :] = jnp.zeros((4, WINDOW, LANES), BF16)
        v_s[:, 0:WINDOW, :] = jnp.zeros((4, WINDOW, LANES), BF16)
        u_s[0:SUBLANES, :] = jnp.zeros((SUBLANES, CONV_WIDTH), F32)

    @pl.when(t > 0)
    def _():
        k_s[:, 0:WINDOW, :] = k_s[:, ts:ts + WINDOW, :]
        v_s[:, 0:WINDOW, :] = v_s[:, ts:ts + WINDOW, :]
        u_s[0:SUBLANES, :] = u_s[ts:ts + SUBLANES, :]

    x = x_ref[...]
    if moe_in:
        x = _moe_combine(x, ya_ref, yb_ref, rtp_ref)
    h = _rms(x, anw_ref[...]).astype(BF16)
    proj = _dot(h, win_ref[:, 0:V_END])

    cosf = cos_ref[...]
    sinf = sin_ref[...]
    lane = lax.broadcasted_iota(jnp.int32, (ts, LANES), 1)
    first_half = (lane & (HEAD_DIM // 2)) == 0
    low_head = lane < HEAD_DIM

    def rope(tq):
        rot = jnp.where(first_half, pltpu.roll(tq, LANES - HEAD_DIM // 2, 1), pltpu.roll(tq, HEAD_DIM // 2, 1))
        return tq * cosf + rot * sinf

    scale = HEAD_DIM ** -0.5
    for p in range(ATTN_WIDTH // LANES):
        q_s[:, p * LANES:(p + 1) * LANES] = (rope(proj[:, p * LANES:(p + 1) * LANES]) * scale).astype(BF16)

    def lane_variants(a, dst):
        ar = pltpu.roll(a, HEAD_DIM, 1)
        zero = jnp.zeros_like(a)
        dst[0, WINDOW:WINDOW + ts, :] = jnp.where(low_head, a, zero).astype(BF16)
        dst[1, WINDOW:WINDOW + ts, :] = jnp.where(low_head, zero, ar).astype(BF16)
        dst[2, WINDOW:WINDOW + ts, :] = jnp.where(low_head, ar, zero).astype(BF16)
        dst[3, WINDOW:WINDOW + ts, :] = jnp.where(low_head, zero, a).astype(BF16)

    lane_variants(rope(proj[:, Q_END:K_END]), k_s)
    lane_variants(proj[:, K_END:V_END], v_s)

    qi = lax.broadcasted_iota(jnp.int32, (WINDOW, 2 * WINDOW), 0)
    kj = lax.broadcasted_iota(jnp.int32, (WINDOW, 2 * WINDOW), 1)
    band = (kj > qi) & (kj <= qi + WINDOW)
    band_first = band & (kj >= jnp.where(t > 0, 0, WINDOW))

    for b in range(nblk):
        mask = band_first if b == 0 else band
        rows = slice(b * WINDOW, (b + 1) * WINDOW)
        ctx = slice(b * WINDOW, (b + 2) * WINDOW)
        for g in range(N_KV_HEADS):
            cchunk = b * N_KV_HEADS + g
            if cchunk < 3 * CONV_WIDTH // MXU_DIM:
                ccols = slice(cchunk * MXU_DIM, (cchunk + 1) * MXU_DIM)
                c_s[:, ccols] = _dot(h, win_ref[:, V_END + cchunk * MXU_DIM:V_END + (cchunk + 1) * MXU_DIM])
            qg =jnp.concatenate([q_s[rows, (2 * g) * LANES:(2 * g + 1) * LANES],
                                  q_s[rows, (2 * g + 1) * LANES:(2 * g + 2) * LANES]], axis=0)
            probs = []
            for j in range(2):
                s = _dot_nt(qg, k_s[2 * g + j, ctx, :])
                halves = []
                for r in range(2):
                    sink = sinks_ref[4 * g + 2 * r + j]
                    sr = jnp.where(mask, s[r * WINDOW:(r + 1) * WINDOW], NEG_INF)
                    m = jnp.maximum(jnp.max(sr, axis=-1, keepdims=True), sink)
                    e = jnp.exp(sr - m)
                    denom = jnp.sum(e, axis=-1, keepdims=True) + jnp.exp(sink - m)
                    halves.append((e * (1.0 / denom)).astype(BF16))
                probs.append(jnp.concatenate(halves, axis=0))
            o = _dot(probs[0], v_s[2 * g, ctx, :]) + _dot(probs[1], v_s[2 * g + 1, ctx, :])
            y_s[rows, (2 * g) * LANES:(2 * g + 1) * LANES] = o[0:WINDOW]
            y_s[rows, (2 * g + 1) * LANES:(2 * g + 2) * LANES] = o[WINDOW:2 * WINDOW]

    for cchunk in range(nblk * N_KV_HEADS, 3 * CONV_WIDTH // MXU_DIM):
        c_s[:, cchunk * MXU_DIM:(cchunk + 1) * MXU_DIM] = _dot(
            h, win_ref[:, V_END + cchunk * MXU_DIM:V_END + (cchunk + 1) * MXU_DIM])
    cproj = c_s[...]
    u_s[SUBLANES:SUBLANES + ts, :] = cproj[:, CONV_WIDTH:2 * CONV_WIDTH] * cproj[:, 2 * CONV_WIDTH:3 * CONV_WIDTH]
    cw = convw_ref[...]
    conv = cw[0:1, :] * u_s[SUBLANES - 2:SUBLANES - 2 + ts, :]
    conv = conv + cw[1:2, :] * u_s[SUBLANES - 1:SUBLANES - 1 + ts, :]
    conv = conv + cw[2:3, :] * u_s[SUBLANES:SUBLANES + ts, :]
    y_s[:, ATTN_WIDTH:D_MODEL] = cproj[:, 0:CONV_WIDTH] * conv

    y = y_s[...]
    sq = (y * y).astype(BF16)
    gsum = gsum_ref[...]
    ms = jnp.concatenate([_dot(sq[:, c * MXU_DIM:(c + 1) * MXU_DIM], gsum) for c in range(D_MODEL // MXU_DIM)],
                         axis=-1)
    yn = (y * lax.rsqrt(ms + NORM_EPS) * hnw_ref[...]).astype(BF16)

    xn = x + _dot(yn, wout_ref[...])
    xo_ref[...] = xn
    h2 = _rms(xn, fnw_ref[...])
    h2_ref[...] = h2.astype(h2_ref.dtype)

    if router:
        lg = _dot(h2.astype(BF16), rw_ref[...])
        lanef = lane.astype(F32)
        l1 = jnp.where(lane < N_EXPERTS, lg, -jnp.inf)
        m1 = jnp.max(l1, axis=-1, keepdims=True)
        i1 = jnp.min(jnp.where(l1 == m1, lanef, float(LANES)), axis=-1, keepdims=True)
        l2 = jnp.where(lanef == i1, -jnp.inf, l1)
        m2 = jnp.max(l2, axis=-1, keepdims=True)
        i2 = jnp.min(jnp.where(l2 == m2, lanef, float(LANES)), axis=-1, keepdims=True)
        e2 = jnp.exp(m2 - m1)
        inv = 1.0 / (1.0 + e2)
        out = jnp.where(lane == 0, i1, jnp.where(lane == 1, i2, jnp.where(lane == 2, inv, e2 * inv)))
        rt_ref[...] = jnp.where(lane < 4, out, 0.0)


def _resident(shape, layer=None):
    if layer is None:
        return pl.BlockSpec(shape, lambda *_: (0,) * len(shape), pipeline_mode=pl.Buffered(1))
    return pl.BlockSpec((None,) + shape, lambda *_: (layer,) + (0,) * len(shape), pipeline_mode=pl.Buffered(1))


def _mixer(x, moe, cosf, sinf, anw, win, sinks, convw, hnw, wout, fnw, gsum, rw, layer, seq, h2_dtype):
    n = x.shape[0]
    ts = min(SEQ_TILE, seq)
    nt = seq // ts
    nb = n // seq
    moe_in = moe is not None
    router = rw is not None
    row = lambda b, t: (b * nt + t, 0)
    tile = pl.BlockSpec((ts, D_MODEL), row)
    lanes_tile = pl.BlockSpec((ts, LANES), row)
    in_specs = [tile]
    args = [x]
    if moe_in:
        ymoe, rt_prev = moe
        off = n // ts
        in_specs += [tile, pl.BlockSpec((ts, D_MODEL), lambda b, t: (off + b * nt + t, 0)), lanes_tile]
        args += [ymoe, ymoe, rt_prev]
    in_specs += [lanes_tile, lanes_tile,
                 _resident((1, D_MODEL), layer), _resident((D_MODEL, IN_WIDTH), layer),
                 pl.BlockSpec(memory_space=pltpu.SMEM),
                 _resident((CONV_K, CONV_WIDTH), layer), _resident((1, D_MODEL), layer),
                 _resident((D_MODEL, D_MODEL), layer), _resident((1, D_MODEL), layer),
                 _resident((MXU_DIM, MXU_DIM))]
    args += [cosf, sinf, anw, win, sinks, convw, hnw, wout, fnw, gsum]
    out_shape = [jax.ShapeDtypeStruct((n, D_MODEL), F32), jax.ShapeDtypeStruct((n, D_MODEL), h2_dtype)]
    out_specs = [tile, tile]
    if router:
        in_specs.append(_resident((D_MODEL, LANES)))
        args.append(rw)
        out_shape.append(jax.ShapeDtypeStruct((n, LANES), F32))
        out_specs.append(lanes_tile)
    return pl.pallas_call(
        functools.partial(_mixer_kernel, ts, moe_in, router),
        grid=(nb, nt),
        in_specs=in_specs,
        out_specs=out_specs,
        out_shape=out_shape,
        scratch_shapes=[pltpu.VMEM((ts, ATTN_WIDTH), BF16),
                        pltpu.VMEM((4, WINDOW + ts, LANES), BF16),
                        pltpu.VMEM((4, WINDOW + ts, LANES), BF16),
                        pltpu.VMEM((SUBLANES + ts, CONV_WIDTH), F32),
                        pltpu.VMEM((ts, D_MODEL), F32),
                        pltpu.VMEM((ts, 3 * CONV_WIDTH), F32)],
        compiler_params=pltpu.CompilerParams(dimension_semantics=("arbitrary", "arbitrary"),
                                             vmem_limit_bytes=VMEM_LIMIT_BYTES),
        name="mixer",
    )(*args)


def _swiglu(hb, wg_ref, wu_ref, wd_ref, acc_ref, side_work=None):
    for c in range(D_FF // FF_CHUNK):
        if side_work is not None:
            side_work(c)
        cols = slice(c * FF_CHUNK, (c + 1) * FF_CHUNK)
        g = _dot(hb, wg_ref[:, cols])
        u = _dot(hb, wu_ref[:, cols])
        a = (jax.nn.silu(g) * u).astype(BF16)
        part = _dot(a, wd_ref[cols, :])
        if c == 0:
            acc_ref[...] = part
        elif c < D_FF // FF_CHUNK - 1:
            acc_ref[...] += part
    return acc_ref[...] + part


def _ffn_kernel(x_ref, h_ref, wg_ref, wu_ref, wd_ref, o_ref, acc_ref):
    o_ref[...] = x_ref[...] + _swiglu(h_ref[...], wg_ref, wu_ref, wd_ref, acc_ref)


def _dense_ffn(x, h2, wg, wu, wd, layer):
    n = x.shape[0]
    tm = min(FFN_TILE, n)
    tile = pl.BlockSpec((tm, D_MODEL), lambda i: (i, 0))
    return pl.pallas_call(
        _ffn_kernel,
        grid=(n // tm,),
        in_specs=[tile, tile, _resident((D_MODEL, D_FF), layer), _resident((D_MODEL, D_FF), layer),
                  _resident((D_FF, D_MODEL), layer)],
        out_specs=tile,
        out_shape=jax.ShapeDtypeStruct((n, D_MODEL), F32),
        scratch_shapes=[pltpu.VMEM((tm, D_MODEL), F32)],
        compiler_params=pltpu.CompilerParams(dimension_semantics=("arbitrary",),
                                             vmem_limit_bytes=VMEM_LIMIT_BYTES),
        name="dense_ffn",
    )(x, h2, wg, wu, wd)


def _moe_kernel(blk_e_ref, src0_ref, srcn_ref, dstp_ref, dstl_ref, h_hbm, wg_ref, wu_ref, wd_ref, y_hbm,
                xbuf, ybuf, acc_ref, gsem, ssem):
    i = pl.program_id(0)
    n_real = blk_e_ref[pl.num_programs(0)]
    slot = i % 2
    other = 1 - slot
    bm = MOE_BLOCK
    per_chunk = bm // MOE_ISSUE_CHUNKS

    def gather_row(idx_ref, r, block_slot):
        pltpu.make_async_copy(h_hbm.at[pl.ds(idx_ref[0, 0, r], 1)], xbuf.at[block_slot, pl.ds(r, 1)],
                              gsem.at[block_slot]).start(priority=0)

    def scatter_row(idx_ref, r, block_slot):
        pltpu.make_async_copy(ybuf.at[block_slot, pl.ds(r, 1)], y_hbm.at[pl.ds(idx_ref[0, 0, r], 1)],
                              ssem.at[block_slot]).start(priority=1)

    def wait_gather(block_slot):
        pltpu.make_async_copy(h_hbm.at[pl.ds(0, bm)], xbuf.at[block_slot], gsem.at[block_slot]).wait()

    def wait_scatter(block_slot):
        pltpu.make_async_copy(ybuf.at[block_slot], y_hbm.at[pl.ds(0, bm)], ssem.at[block_slot]).wait()

    @pl.when(i == 0)
    def _():
        def body(r, c):
            gather_row(src0_ref, r, 0)
            return c
        lax.fori_loop(0, bm, body, 0)
        ybuf[...] = jnp.zeros((2, bm, D_MODEL), F32)
        pltpu.make_async_copy(ybuf.at[0], y_hbm.at[pl.ds(y_hbm.shape[0] - bm, bm)], ssem.at[0]).start()

    @pl.when(i < n_real)
    def _():
        wait_gather(slot)

        def side_work(c):
            if c < MOE_ISSUE_CHUNKS:
                for r in range(c * per_chunk, (c + 1) * per_chunk):
                    gather_row(srcn_ref, r, other)
                    scatter_row(dstp_ref, r, other)

        total = _swiglu(xbuf[slot].astype(BF16), wg_ref, wu_ref, wd_ref, acc_ref, side_work)
        wait_scatter(slot)
        ybuf[slot] = total

    @pl.when(i >= n_real)
    def _():
        ybuf[0] = jnp.zeros((bm, D_MODEL), F32)
        fill = pltpu.make_async_copy(ybuf.at[0], y_hbm.at[pl.ds(pl.multiple_of(i * bm, bm), bm)], ssem.at[0])
        fill.start()
        fill.wait()

    @pl.when(i == n_real - 1)
    def _():
        def body(r, c):
            scatter_row(dstl_ref, r, slot)
            return c
        lax.fori_loop(0, bm, body, 0)
        wait_scatter(slot)
        wait_scatter(other)
        wait_gather(other)


def _moe_experts(h2, blk_e, src, dst, wg, wu, wd, layer):
    nblocks = src.shape[0]
    bm = MOE_BLOCK
    n_rows = (nblocks + 2) * bm
    wspec_in = pl.BlockSpec((None, None, D_MODEL, D_FF), lambda i, be: (layer, be[i], 0, 0),
                            pipeline_mode=pl.Buffered(1))
    wspec_out = pl.BlockSpec((None, None, D_FF, D_MODEL), lambda i, be: (layer, be[i], 0, 0),
                             pipeline_mode=pl.Buffered(1))
    smem_rows = lambda index_map: pl.BlockSpec((1, 1, bm), index_map, memory_space=pltpu.SMEM)
    grid_spec = pltpu.PrefetchScalarGridSpec(
        num_scalar_prefetch=1,
        grid=(nblocks,),
        in_specs=[smem_rows(lambda i, be: (0, 0, 0)),
                  smem_rows(lambda i, be: (jnp.minimum(i + 1, nblocks - 1), 0, 0)),
                  smem_rows(lambda i, be: (i, 0, 0)),
                  smem_rows(lambda i, be: (be[nblocks], 0, 0)),
                  pl.BlockSpec(memory_space=pl.ANY),
                  wspec_in, wspec_in, wspec_out],
        out_specs=pl.BlockSpec(memory_space=pl.ANY),
        scratch_shapes=[pltpu.VMEM((2, bm, D_MODEL), F32),
                        pltpu.VMEM((2, bm, D_MODEL), F32),
                        pltpu.VMEM((bm, D_MODEL), F32),
                        pltpu.SemaphoreType.DMA((2,)),
                        pltpu.SemaphoreType.DMA((2,))],
    )
    return pl.pallas_call(
        _moe_kernel,
        grid_spec=grid_spec,
        out_shape=jax.ShapeDtypeStruct((n_rows, D_MODEL), F32),
        compiler_params=pltpu.CompilerParams(dimension_semantics=("arbitrary",),
                                             vmem_limit_bytes=VMEM_LIMIT_BYTES),
        name="moe_experts",
    )(blk_e, src, src, dst, dst, h2, wg, wu, wd)


def _route(rt, n):
    bm = MOE_BLOCK
    n_assign = n * TOP_K
    n_rows = n_assign + N_EXPERTS * bm
    nblocks = n_rows // bm
    flat_e = rt[:, 0:TOP_K].astype(jnp.int32).reshape(-1)
    order = jnp.argsort(flat_e).astype(jnp.int32)
    experts = jnp.arange(N_EXPERTS, dtype=jnp.int32)
    sizes = jnp.sum((flat_e[:, None] == experts[None, :]).astype(jnp.int32), axis=0)
    padded = (sizes + bm - 1) // bm * bm
    group_end = jnp.cumsum(sizes)
    padded_end = jnp.cumsum(padded)
    blk_start = jnp.arange(nblocks, dtype=jnp.int32) * bm
    blk_e = jnp.minimum(jnp.sum((blk_start[:, None] >= padded_end[None, :]).astype(jnp.int32), axis=1),
                        N_EXPERTS - 1)
    rank = blk_start[:, None] + jnp.arange(bm, dtype=jnp.int32)[None, :] - (padded_end - padded)[blk_e][:, None]
    valid = rank < sizes[blk_e][:, None]
    sorted_pos = (group_end - sizes)[blk_e][:, None] + rank
    assign = order[jnp.clip(sorted_pos, 0, n_assign - 1)]
    tok = assign // TOP_K
    src = jnp.where(valid, tok, 0)
    slot = blk_start[:, None] + jnp.arange(bm, dtype=jnp.int32)[None, :]
    dst = jnp.where(valid, (assign % TOP_K) * n + tok, n_assign + slot - group_end[blk_e][:, None])
    spare = n_rows + jnp.arange(bm, dtype=jnp.int32)
    dst = jnp.concatenate([spare[None, :], dst], axis=0)
    n_real = padded_end[-1:] // bm
    return jnp.concatenate([blk_e, n_real]), src.reshape(nblocks, 1, bm), dst.reshape(nblocks + 1, 1, bm)


def _final_kernel(moe_in, *refs):
    if moe_in:
        x_ref, ya_ref, yb_ref, rt_ref, w_ref, o_ref = refs
        x = _moe_combine(x_ref[...], ya_ref, yb_ref, rt_ref)
    else:
        x_ref, w_ref, o_ref = refs
        x = x_ref[...]
    o_ref[...] = _rms(x, w_ref[...])


def _final(x, moe, w):
    n = x.shape[0]
    tm = min(1024, n)
    off = n // tm
    tile = pl.BlockSpec((tm, D_MODEL), lambda i: (i, 0))
    in_specs = [tile]
    args = [x]
    if moe is not None:
        in_specs += [tile, pl.BlockSpec((tm, D_MODEL), lambda i: (off + i, 0)),
                     pl.BlockSpec((tm, LANES), lambda i: (i, 0))]
        args += [moe[0], moe[0], moe[1]]
    return pl.pallas_call(
        functools.partial(_final_kernel, moe is not None),
        grid=(n // tm,),
        in_specs=in_specs + [_resident((1, D_MODEL))],
        out_specs=tile,
        out_shape=jax.ShapeDtypeStruct((n, D_MODEL), F32),
        compiler_params=pltpu.CompilerParams(dimension_semantics=("arbitrary",)),
        name="final_norm",
    )(*args, w)


def kernel(x, positions, attn_norm_w, w_in, attn_sinks, conv_w, head_norm_w, w_out, ffn_norm_w, dense_w_gate,
           dense_w_up, dense_w_down, router_w, moe_w_gate, moe_w_up, moe_w_down, final_norm_w):
    bsz, seq, _ = x.shape
    n = bsz * seq
    depth = w_in.shape[0]

    inv_freq = ROPE_THETA ** (-jnp.arange(0, HEAD_DIM, 2, dtype=F32) / HEAD_DIM)
    ang = positions.astype(F32)[..., None] * inv_freq
    cos, sin = jnp.cos(ang), jnp.sin(ang)
    cosf = jnp.tile(cos, (1, 1, 4)).reshape(n, LANES)
    sinf = jnp.tile(jnp.concatenate([-sin, sin], axis=-1), (1, 1, 2)).reshape(n, LANES)

    grp = jnp.arange(MXU_DIM) // HEAD_DIM
    gsum = jnp.where(grp[:, None] == grp[None, :], 1.0 / HEAD_DIM, 0.0).astype(BF16)

    win_b, wout_b = w_in.astype(BF16), w_out.astype(BF16)
    dense_b = [w.astype(BF16) for w in (dense_w_gate, dense_w_up, dense_w_down)]
    moe_b = [w.astype(BF16) for w in (moe_w_gate, moe_w_up, moe_w_down)]
    rw_b = jnp.pad(router_w, ((0, 0), (0, 0), (0, LANES - N_EXPERTS))).astype(BF16)
    anw, hnw, fnw = (w[:, None, :] for w in (attn_norm_w, head_norm_w, ffn_norm_w))

    xf = x.reshape(n, D_MODEL)
    moe = None
    for l in range(depth):
        is_moe = l % 2 == 1
        i = l // 2
        outs = _mixer(xf, moe, cosf, sinf, anw, win_b, attn_sinks[l], conv_w, hnw, wout_b, fnw, gsum,
                      rw_b[i] if is_moe else None, l, seq, F32 if is_moe else BF16)
        if is_moe:
            xf, h2, rt = outs
            blk_e, src, dst = _route(rt, n)
            moe = (_moe_experts(h2, blk_e, src, dst, *moe_b, i), rt)
        else:
            xf, h2 = outs
            xf = _dense_ffn(xf, h2, *dense_b, i)
            moe = None
    out = _final(xf, moe, final_norm_w[None])
    return out.reshape(bsz, seq, D_MODEL)
```

```python
import functools

import jax
import jax.numpy as jnp
from jax import lax
from jax.experimental import pallas as pl
from jax.experimental.pallas import tpu as pltpu

D_MODEL = 1024
HEAD_DIM = 64
N_Q_HEADS = 8
N_KV_HEADS = 2
ATTN_WIDTH = N_Q_HEADS * HEAD_DIM
KV_WIDTH = N_KV_HEADS * HEAD_DIM
CONV_WIDTH = D_MODEL - ATTN_WIDTH
Q_END = ATTN_WIDTH
K_END = Q_END + KV_WIDTH
V_END = K_END + KV_WIDTH
CB_END = V_END + CONV_WIDTH
CC_END = CB_END + CONV_WIDTH
IN_WIDTH = CC_END + CONV_WIDTH
WINDOW = 128
ROPE_THETA = 10000.0
CONV_K = 3
D_FF = 3584
N_EXPERTS = 8
TOP_K = 2
NORM_EPS = 1e-5
NEG_INF = -1e30

LANES = 128
SUBLANES = 8
MXU_DIM = 256
VMEM_LIMIT_BYTES = 58 * 1024 * 1024

SEQ_TILE = 512
FFN_TILE = 512
FF_CHUNK = 512
MOE_BLOCK = 512
MOE_ISSUE_CHUNKS = 4

F32 = jnp.float32
BF16 = jnp.bfloat16


def _rms(x, w):
    var = jnp.mean(x * x, axis=-1, keepdims=True)
    return x * lax.rsqrt(var + NORM_EPS) * w


def _dot(a, b):
    return jnp.dot(a, b, preferred_element_type=F32)


def _dot_nt(a, b):
    return lax.dot_general(a, b, (((1,), (1,)), ((), ())), preferred_element_type=F32)


def _moe_combine(x, ya_ref, yb_ref, rt_ref):
    rt = rt_ref[...]
    return x + rt[:, TOP_K:TOP_K + 1] * ya_ref[...] + rt[:, TOP_K + 1:TOP_K + 2] * yb_ref[...]


def _mixer_kernel(ts, moe_in, router, *refs):
    refs = list(refs)
    x_ref = refs.pop(0)
    ya_ref = refs.pop(0) if moe_in else None
    yb_ref = refs.pop(0) if moe_in else None
    rtp_ref = refs.pop(0) if moe_in else None
    (cos_ref, sin_ref, anw_ref, win_ref, sinks_ref, convw_ref, hnw_ref, wout_ref, fnw_ref,
     gsum_ref) = refs[:10]
    refs = refs[10:]
    rw_ref = refs.pop(0) if router else None
    xo_ref = refs.pop(0)
    h2_ref = refs.pop(0)
    rt_ref = refs.pop(0) if router else None
    q_s, k_s, v_s, u_s, y_s, c_s = refs

    t = pl.program_id(1)
    nblk = ts // WINDOW

    @pl.when(t == 0)
    def _():
        k_s[:, 0:WINDOW, :] = jnp.zeros((4, WINDOW, LANES), BF16)
        v_s[:, 0:WINDOW, :] = jnp.zeros((4, WINDOW, LANES), BF16)
        u_s[0:SUBLANES, :] = jnp.zeros((SUBLANES, CONV_WIDTH), F32)

    @pl.when(t > 0)
    def _():
        k_s[:, 0:WINDOW, :] = k_s[:, ts:ts + WINDOW, :]
        v_s[:, 0:WINDOW, :] = v_s[:, ts:ts + WINDOW, :]
        u_s[0:SUBLANES, :] = u_s[ts:ts + SUBLANES, :]

    x = x_ref[...]
    if moe_in:
        x = _moe_combine(x, ya_ref, yb_ref, rtp_ref)
    h = _rms(x, anw_ref[...]).astype(BF16)
    proj = _dot(h, win_ref[:, 0:V_END])

    cosf = cos_ref[...]
    sinf = sin_ref[...]
    lane = lax.broadcasted_iota(jnp.int32, (ts, LANES), 1)
    first_half = (lane & (HEAD_DIM // 2)) == 0
    low_head = lane < HEAD_DIM

    def rope(tq):
        rot = jnp.where(first_half, pltpu.roll(tq, LANES - HEAD_DIM // 2, 1), pltpu.roll(tq, HEAD_DIM // 2, 1))
        return tq * cosf + rot * sinf

    scale = HEAD_DIM ** -0.5
    for p in range(ATTN_WIDTH // LANES):
        q_s[:, p * LANES:(p + 1) * LANES] = (rope(proj[:, p * LANES:(p + 1) * LANES]) * scale).astype(BF16)

    def lane_variants(a, dst):
        ar = pltpu.roll(a, HEAD_DIM, 1)
        zero = jnp.zeros_like(a)
        dst[0, WINDOW:WINDOW + ts, :] = jnp.where(low_head, a, zero).astype(BF16)
        dst[1, WINDOW:WINDOW + ts, :] = jnp.where(low_head, zero, ar).astype(BF16)
        dst[2, WINDOW:WINDOW + ts, :] = jnp.where(low_head, ar, zero).astype(BF16)
        dst[3, WINDOW:WINDOW + ts, :] = jnp.where(low_head, zero, a).astype(BF16)

    lane_variants(rope(proj[:, Q_END:K_END]), k_s)
    lane_variants(proj[:, K_END:V_END], v_s)

    qi = lax.broadcasted_iota(jnp.int32, (WINDOW, 2 * WINDOW), 0)
    kj = lax.broadcasted_iota(jnp.int32, (WINDOW, 2 * WINDOW), 1)
    band = (kj > qi) & (kj <= qi + WINDOW)
    band_first = band & (kj >= jnp.where(t > 0, 0, WINDOW))

    for b in range(nblk):
        mask = band_first if b == 0 else band
        rows = slice(b * WINDOW, (b + 1) * WINDOW)
        ctx = slice(b * WINDOW, (b + 2) * WINDOW)
        for g in range(N_KV_HEADS):
            cchunk = b * N_KV_HEADS + g
            if cchunk < 3 * CONV_WIDTH // MXU_DIM:
                ccols = slice(cchunk * MXU_DIM, (cchunk + 1) * MXU_DIM)
                c_s[:, ccols] = _dot(h, win_ref[:, V_END + cchunk * MXU_DIM:V_END + (cchunk + 1) * MXU_DIM])
            qg = jnp.concatenate([q_s[rows, (2 * g) * LANES:(2 * g + 1) * LANES],
                                  q_s[rows, (2 * g + 1) * LANES:(2 * g + 2) * LANES]], axis=0)
            probs = []
            for j in range(2):
                s = _dot_nt(qg, k_s[2 * g + j, ctx, :])
                halves = []
                for r in range(2):
                    sink = sinks_ref[4 * g + 2 * r + j]
                    sr = jnp.where(mask, s[r * WINDOW:(r + 1) * WINDOW], NEG_INF)
                    m = jnp.maximum(jnp.max(sr, axis=-1, keepdims=True), sink)
                    e = jnp.exp(sr - m)
                    denom = jnp.sum(e, axis=-1, keepdims=True) + jnp.exp(sink - m)
                    halves.append((e * (1.0 / denom)).astype(BF16))
                probs.append(jnp.concatenate(halves, axis=0))
            o = _dot(probs[0], v_s[2 * g, ctx, :]) + _dot(probs[1], v_s[2 * g + 1, ctx, :])
            y_s[rows, (2 * g) * LANES:(2 * g + 1) * LANES] = o[0:WINDOW]
            y_s[rows, (2 * g + 1) * LANES:(2 * g + 2) * LANES] = o[WINDOW:2 * WINDOW]

    for cchunk in range(nblk * N_KV_HEADS, 3 * CONV_WIDTH // MXU_DIM):
        c_s[:, cchunk * MXU_DIM:(cchunk + 1) * MXU_DIM] = _dot(
            h, win_ref[:, V_END + cchunk * MXU_DIM:V_END + (cchunk + 1) * MXU_DIM])
    cproj = c_s[...]
    u_s[SUBLANES:SUBLANES + ts, :] = cproj[:, CONV_WIDTH:2 * CONV_WIDTH] * cproj[:, 2 * CONV_WIDTH:3 * CONV_WIDTH]
    cw = convw_ref[...]
    conv = cw[0:1, :] * u_s[SUBLANES - 2:SUBLANES - 2 + ts, :]
    conv = conv + cw[1:2, :] * u_s[SUBLANES - 1:SUBLANES - 1 + ts, :]
    conv = conv + cw[2:3, :] * u_s[SUBLANES:SUBLANES + ts, :]
    y_s[:, ATTN_WIDTH:D_MODEL] = cproj[:, 0:CONV_WIDTH] * conv

    y = y_s[...]
    sq = (y * y).astype(BF16)
    gsum = gsum_ref[...]
    ms = jnp.concatenate([_dot(sq[:, c * MXU_DIM:(c + 1) * MXU_DIM], gsum) for c in range(D_MODEL // MXU_DIM)],
                         axis=-1)
    yn = (y * lax.rsqrt(ms + NORM_EPS) * hnw_ref[...]).astype(BF16)

    xn = x + _dot(yn, wout_ref[...])
    xo_ref[...] = xn
    h2 = _rms(xn, fnw_ref[...])
    h2_ref[...] = h2.astype(h2_ref.dtype)

    if router:
        lg = _dot(h2.astype(BF16), rw_ref[...])
        lanef = lane.astype(F32)
        l1 = jnp.where(lane < N_EXPERTS, lg, -jnp.inf)
        m1 = jnp.max(l1, axis=-1, keepdims=True)
        i1 = jnp.min(jnp.where(l1 == m1, lanef, float(LANES)), axis=-1, keepdims=True)
        l2 = jnp.where(lanef == i1, -jnp.inf, l1)
        m2 = jnp.max(l2, axis=-1, keepdims=True)
        i2 = jnp.min(jnp.where(l2 == m2, lanef, float(LANES)), axis=-1, keepdims=True)
        e2 = jnp.exp(m2 - m1)
        inv = 1.0 / (1.0 + e2)
        out = jnp.where(lane == 0, i1, jnp.where(lane == 1, i2, jnp.where(lane == 2, inv, e2 * inv)))
        rt_ref[...] = jnp.where(lane < 4, out, 0.0)


def _resident(shape, layer=None):
    if layer is None:
        return pl.BlockSpec(shape, lambda *_: (0,) * len(shape), pipeline_mode=pl.Buffered(1))
    return pl.BlockSpec((None,) + shape, lambda *_: (layer,) + (0,) * len(shape), pipeline_mode=pl.Buffered(1))


def _mixer(x, moe, cosf, sinf, anw, win, sinks, convw, hnw, wout, fnw, gsum, rw, layer, seq, h2_dtype):
    n = x.shape[0]
    ts = min(SEQ_TILE, seq)
    nt = seq // ts
    nb = n // seq
    moe_in = moe is not None
    router = rw is not None
    row = lambda b, t: (b * nt + t, 0)
    tile = pl.BlockSpec((ts, D_MODEL), row)
    lanes_tile = pl.BlockSpec((ts, LANES), row)
    in_specs = [tile]
    args = [x]
    if moe_in:
        ymoe, rt_prev = moe
        off = n // ts
        in_specs += [tile, pl.BlockSpec((ts, D_MODEL), lambda b, t: (off + b * nt + t, 0)), lanes_tile]
        args += [ymoe, ymoe, rt_prev]
    in_specs += [lanes_tile, lanes_tile,
                 _resident((1, D_MODEL), layer), _resident((D_MODEL, IN_WIDTH), layer),
                 pl.BlockSpec(memory_space=pltpu.SMEM),
                 _resident((CONV_K, CONV_WIDTH), layer), _resident((1, D_MODEL), layer),
                 _resident((D_MODEL, D_MODEL), layer), _resident((1, D_MODEL), layer),
                 _resident((MXU_DIM, MXU_DIM))]
    args += [cosf, sinf, anw, win, sinks, convw, hnw, wout, fnw, gsum]
    out_shape = [jax.ShapeDtypeStruct((n, D_MODEL), F32), jax.ShapeDtypeStruct((n, D_MODEL), h2_dtype)]
    out_specs = [tile, tile]
    if router:
        in_specs.append(_resident((D_MODEL, LANES)))
        args.append(rw)
        out_shape.append(jax.ShapeDtypeStruct((n, LANES), F32))
        out_specs.append(lanes_tile)
    return pl.pallas_call(
        functools.partial(_mixer_kernel, ts, moe_in, router),
        grid=(nb, nt),
        in_specs=in_specs,
        out_specs=out_specs,
        out_shape=out_shape,
        scratch_shapes=[pltpu.VMEM((ts, ATTN_WIDTH), BF16),
                        pltpu.VMEM((4, WINDOW + ts, LANES), BF16),
                        pltpu.VMEM((4, WINDOW + ts, LANES), BF16),
                        pltpu.VMEM((SUBLANES + ts, CONV_WIDTH), F32),
                        pltpu.VMEM((ts, D_MODEL), F32),
                        pltpu.VMEM((ts, 3 * CONV_WIDTH), F32)],
        compiler_params=pltpu.CompilerParams(dimension_semantics=("arbitrary", "arbitrary"),
                                             vmem_limit_bytes=VMEM_LIMIT_BYTES),
        name="mixer",
    )(*args)


def _swiglu(hb, wg_ref, wu_ref, wd_ref, acc_ref, side_work=None):
    for c in range(D_FF // FF_CHUNK):
        if side_work is not None:
            side_work(c)
        cols = slice(c * FF_CHUNK, (c + 1) * FF_CHUNK)
        g = _dot(hb, wg_ref[:, cols])
        u = _dot(hb, wu_ref[:, cols])
        a = (jax.nn.silu(g) * u).astype(BF16)
        part = _dot(a, wd_ref[cols, :])
        if c == 0:
            acc_ref[...] = part
        elif c < D_FF // FF_CHUNK - 1:
            acc_ref[...] += part
    return acc_ref[...] + part


def _ffn_kernel(n_cast, x_ref, h_ref, wg_ref, wu_ref, wd_ref, *refs):
    cast_in, o_ref, cast_out, acc_ref = refs[:n_cast], refs[n_cast], refs[n_cast + 1:-1], refs[-1]
    o_ref[...] = x_ref[...] + _swiglu(h_ref[...], wg_ref, wu_ref, wd_ref, acc_ref)
    for src, dst in zip(cast_in, cast_out):
        dst[...] = src[...].astype(dst.dtype)


def _dense_ffn(x, h2, wg, wu, wd, layer, cast=None):
    n = x.shape[0]
    tm = min(FFN_TILE, n)
    steps = n // tm
    tile = pl.BlockSpec((tm, D_MODEL), lambda i: (i, 0))
    in_specs = [tile, tile, _resident((D_MODEL, D_FF), layer), _resident((D_MODEL, D_FF), layer),
                _resident((D_FF, D_MODEL), layer)]
    out_specs = [tile]
    out_shape = [jax.ShapeDtypeStruct((n, D_MODEL), F32)]
    args = [x, h2, wg, wu, wd]
    if cast is not None:
        weights, cast_layer = cast
        per_expert = steps // N_EXPERTS
        assert per_expert * N_EXPERTS == steps
        for w in weights:
            rows, cols = w.shape[2] // per_expert, w.shape[3]
            in_specs.append(pl.BlockSpec((None, None, rows, cols),
                                         lambda i: (cast_layer, i // per_expert, i % per_expert, 0)))
            out_specs.append(pl.BlockSpec((None, rows, cols), lambda i: (i // per_expert, i % per_expert, 0)))
            out_shape.append(jax.ShapeDtypeStruct(w.shape[1:], BF16))
            args.append(w)
    outs = pl.pallas_call(
        functools.partial(_ffn_kernel, len(args) - 5),
        grid=(steps,),
        in_specs=in_specs,
        out_specs=out_specs,
        out_shape=out_shape,
        scratch_shapes=[pltpu.VMEM((tm, D_MODEL), F32)],
        compiler_params=pltpu.CompilerParams(dimension_semantics=("arbitrary",),
                                             vmem_limit_bytes=VMEM_LIMIT_BYTES),
        name="dense_ffn",
    )(*args)
    return outs[0], outs[1:]


def _moe_kernel(blk_e_ref, src0_ref, srcn_ref, dstp_ref, dstl_ref, h_hbm, wg_ref, wu_ref, wd_ref, y_hbm,
                xbuf, ybuf, acc_ref, gsem, ssem):
    i = pl.program_id(0)
    n_real = blk_e_ref[pl.num_programs(0)]
    slot = i % 2
    other = 1 - slot
    bm = MOE_BLOCK
    per_chunk = bm // MOE_ISSUE_CHUNKS

    def gather_row(idx_ref, r, block_slot):
        pltpu.make_async_copy(h_hbm.at[pl.ds(idx_ref[0, 0, r], 1)], xbuf.at[block_slot, pl.ds(r, 1)],
                              gsem.at[block_slot]).start(priority=0)

    def scatter_row(idx_ref, r, block_slot):
        pltpu.make_async_copy(ybuf.at[block_slot, pl.ds(r, 1)], y_hbm.at[pl.ds(idx_ref[0, 0, r], 1)],
                              ssem.at[block_slot]).start(priority=1)

    def wait_gather(block_slot):
        pltpu.make_async_copy(h_hbm.at[pl.ds(0, bm)], xbuf.at[block_slot], gsem.at[block_slot]).wait()

    def wait_scatter(block_slot):
        pltpu.make_async_copy(ybuf.at[block_slot], y_hbm.at[pl.ds(0, bm)], ssem.at[block_slot]).wait()

    @pl.when(i == 0)
    def _():
        def body(r, c):
            gather_row(src0_ref, r, 0)
            return c
        lax.fori_loop(0, bm, body, 0)
        ybuf[...] = jnp.zeros((2, bm, D_MODEL), F32)
        pltpu.make_async_copy(ybuf.at[0], y_hbm.at[pl.ds(y_hbm.shape[0] - bm, bm)], ssem.at[0]).start()

    @pl.when(i < n_real)
    def _():
        wait_gather(slot)

        def side_work(c):
            if c < MOE_ISSUE_CHUNKS:
                for r in range(c * per_chunk, (c + 1) * per_chunk):
                    gather_row(srcn_ref, r, other)
                    scatter_row(dstp_ref, r, other)

        total = _swiglu(xbuf[slot].astype(BF16), wg_ref, wu_ref, wd_ref, acc_ref, side_work)
        wait_scatter(slot)
        ybuf[slot] = total

    @pl.when(i >= n_real)
    def _():
        ybuf[0] = jnp.zeros((bm, D_MODEL), F32)
        fill = pltpu.make_async_copy(ybuf.at[0], y_hbm.at[pl.ds(pl.multiple_of(i * bm, bm), bm)], ssem.at[0])
        fill.start()
        fill.wait()

    @pl.when(i == n_real - 1)
    def _():
        def body(r, c):
            scatter_row(dstl_ref, r, slot)
            return c
        lax.fori_loop(0, bm, body, 0)
        wait_scatter(slot)
        wait_scatter(other)
        wait_gather(other)


def _moe_experts(h2, blk_e, src, dst, wg, wu, wd):
    nblocks = src.shape[0]
    bm = MOE_BLOCK
    n_rows = (nblocks + 2) * bm
    wspec_in = pl.BlockSpec((None, D_MODEL, D_FF), lambda i, be: (be[i], 0, 0), pipeline_mode=pl.Buffered(1))
    wspec_out = pl.BlockSpec((None, D_FF, D_MODEL), lambda i, be: (be[i], 0, 0), pipeline_mode=pl.Buffered(1))
    smem_rows = lambda index_map: pl.BlockSpec((1, 1, bm), index_map, memory_space=pltpu.SMEM)
    grid_spec = pltpu.PrefetchScalarGridSpec(
        num_scalar_prefetch=1,
        grid=(nblocks,),
        in_specs=[smem_rows(lambda i, be: (0, 0, 0)),
                  smem_rows(lambda i, be: (jnp.minimum(i + 1, nblocks - 1), 0, 0)),
                  smem_rows(lambda i, be: (i, 0, 0)),
                  smem_rows(lambda i, be: (be[nblocks], 0, 0)),
                  pl.BlockSpec(memory_space=pl.ANY),
                  wspec_in, wspec_in, wspec_out],
        out_specs=pl.BlockSpec(memory_space=pl.ANY),
        scratch_shapes=[pltpu.VMEM((2, bm, D_MODEL), F32),
                        pltpu.VMEM((2, bm, D_MODEL), F32),
                        pltpu.VMEM((bm, D_MODEL), F32),
                        pltpu.SemaphoreType.DMA((2,)),
                        pltpu.SemaphoreType.DMA((2,))],
    )
    return pl.pallas_call(
        _moe_kernel,
        grid_spec=grid_spec,
        out_shape=jax.ShapeDtypeStruct((n_rows, D_MODEL), F32),
        compiler_params=pltpu.CompilerParams(dimension_semantics=("arbitrary",),
                                             vmem_limit_bytes=VMEM_LIMIT_BYTES),
        name="moe_experts",
    )(blk_e, src, src, dst, dst, h2, wg, wu, wd)


def _route(rt, n):
    bm = MOE_BLOCK
    n_assign = n * TOP_K
    n_rows = n_assign + N_EXPERTS * bm
    nblocks = n_rows // bm
    flat_e = rt[:, 0:TOP_K].astype(jnp.int32).reshape(-1)
    order = jnp.argsort(flat_e).astype(jnp.int32)
    experts = jnp.arange(N_EXPERTS, dtype=jnp.int32)
    sizes = jnp.sum((flat_e[:, None] == experts[None, :]).astype(jnp.int32), axis=0)
    padded = (sizes + bm - 1) // bm * bm
    group_end = jnp.cumsum(sizes)
    padded_end = jnp.cumsum(padded)
    blk_start = jnp.arange(nblocks, dtype=jnp.int32) * bm
    blk_e = jnp.minimum(jnp.sum((blk_start[:, None] >= padded_end[None, :]).astype(jnp.int32), axis=1),
                        N_EXPERTS - 1)
    rank = blk_start[:, None] + jnp.arange(bm, dtype=jnp.int32)[None, :] - (padded_end - padded)[blk_e][:, None]
    valid = rank < sizes[blk_e][:, None]
    sorted_pos = (group_end - sizes)[blk_e][:, None] + rank
    assign = order[jnp.clip(sorted_pos, 0, n_assign - 1)]
    tok = assign // TOP_K
    src = jnp.where(valid, tok, 0)
    slot = blk_start[:, None] + jnp.arange(bm, dtype=jnp.int32)[None, :]
    dst = jnp.where(valid, (assign % TOP_K) * n + tok, n_assign + slot - group_end[blk_e][:, None])
    spare = n_rows + jnp.arange(bm, dtype=jnp.int32)
    dst = jnp.concatenate([spare[None, :], dst], axis=0)
    n_real = padded_end[-1:] // bm
    return jnp.concatenate([blk_e, n_real]), src.reshape(nblocks, 1, bm), dst.reshape(nblocks + 1, 1, bm)


def _final_kernel(moe_in, *refs):
    if moe_in:
        x_ref, ya_ref, yb_ref, rt_ref, w_ref, o_ref = refs
        x = _moe_combine(x_ref[...], ya_ref, yb_ref, rt_ref)
    else:
        x_ref, w_ref, o_ref = refs
        x = x_ref[...]
    o_ref[...] = _rms(x, w_ref[...])


def _final(x, moe, w):
    n = x.shape[0]
    tm = min(1024, n)
    off = n // tm
    tile = pl.BlockSpec((tm, D_MODEL), lambda i: (i, 0))
    in_specs = [tile]
    args = [x]
    if moe is not None:
        in_specs += [tile, pl.BlockSpec((tm, D_MODEL), lambda i: (off + i, 0)),
                     pl.BlockSpec((tm, LANES), lambda i: (i, 0))]
        args += [moe[0], moe[0], moe[1]]
    return pl.pallas_call(
        functools.partial(_final_kernel, moe is not None),
        grid=(n // tm,),
        in_specs=in_specs + [_resident((1, D_MODEL))],
        out_specs=tile,
        out_shape=jax.ShapeDtypeStruct((n, D_MODEL), F32),
        compiler_params=pltpu.CompilerParams(dimension_semantics=("arbitrary",)),
        name="final_norm",
    )(*args, w)


def kernel(x, positions, attn_norm_w, w_in, attn_sinks, conv_w, head_norm_w, w_out, ffn_norm_w, dense_w_gate,
           dense_w_up, dense_w_down, router_w, moe_w_gate, moe_w_up, moe_w_down, final_norm_w):
    bsz, seq, _ = x.shape
    n = bsz * seq
    depth = w_in.shape[0]

    inv_freq = ROPE_THETA ** (-jnp.arange(0, HEAD_DIM, 2, dtype=F32) / HEAD_DIM)
    ang = positions.astype(F32)[..., None] * inv_freq
    cos, sin = jnp.cos(ang), jnp.sin(ang)
    cosf = jnp.tile(cos, (1, 1, 4)).reshape(n, LANES)
    sinf = jnp.tile(jnp.concatenate([-sin, sin], axis=-1), (1, 1, 2)).reshape(n, LANES)

    grp = jnp.arange(MXU_DIM) // HEAD_DIM
    gsum = jnp.where(grp[:, None] == grp[None, :], 1.0 / HEAD_DIM, 0.0).astype(BF16)

    win_b, wout_b = w_in.astype(BF16), w_out.astype(BF16)
    dense_b = [w.astype(BF16) for w in (dense_w_gate, dense_w_up, dense_w_down)]
    moe_f = (moe_w_gate, moe_w_up, moe_w_down)
    rw_b = jnp.pad(router_w, ((0, 0), (0, 0), (0, LANES - N_EXPERTS))).astype(BF16)
    anw, hnw, fnw = (w[:, None, :] for w in (attn_norm_w, head_norm_w, ffn_norm_w))

    cast_in_dense = (n // min(FFN_TILE, n)) % N_EXPERTS == 0

    xf = x.reshape(n, D_MODEL)
    moe = None
    moe_b = ()
    for l in range(depth):
        is_moe = l % 2 == 1
        i = l // 2
        outs = _mixer(xf, moe, cosf, sinf, anw, win_b, attn_sinks[l], conv_w, hnw, wout_b, fnw, gsum,
                      rw_b[i] if is_moe else None, l, seq, F32 if is_moe else BF16)
        if is_moe:
            xf, h2, rt = outs
            blk_e, src, dst = _route(rt, n)
            if not moe_b:
                moe_b = [w[i].astype(BF16) for w in moe_f]
            moe = (_moe_experts(h2, blk_e, src, dst, *moe_b), rt)
            moe_b = ()
        else:
            xf, h2 = outs
            cast = (moe_f, i) if cast_in_dense and l + 1 < depth else None
            xf, moe_b = _dense_ffn(xf, h2, *dense_b, i, cast)
            moe = None
    out = _final(xf, moe, final_norm_w[None])
    return out.reshape(bsz, seq, D_MODEL)
```

```python
import functools

import jax
import jax.numpy as jnp
from jax import lax
from jax.experimental import pallas as pl
from jax.experimental.pallas import tpu as pltpu

D_MODEL = 1024
HEAD_DIM = 64
N_Q_HEADS = 8
N_KV_HEADS = 2
ATTN_WIDTH = N_Q_HEADS * HEAD_DIM
KV_WIDTH = N_KV_HEADS * HEAD_DIM
CONV_WIDTH = D_MODEL - ATTN_WIDTH
Q_END = ATTN_WIDTH
K_END = Q_END + KV_WIDTH
V_END = K_END + KV_WIDTH
CB_END = V_END + CONV_WIDTH
CC_END = CB_END + CONV_WIDTH
IN_WIDTH = CC_END + CONV_WIDTH
WINDOW = 128
ROPE_THETA = 10000.0
CONV_K = 3
D_FF = 3584
N_EXPERTS = 8
TOP_K = 2
NORM_EPS = 1e-5
NEG_INF = -1e30

LANES = 128
SUBLANES = 8
MXU_DIM = 256
VMEM_LIMIT_BYTES = 58 * 1024 * 1024

SEQ_TILE = 512
FFN_TILE = 512
FF_CHUNK = 1792
MOE_BLOCK = 512
MOE_ISSUE_CHUNKS = min(4, D_FF // FF_CHUNK)

F32 = jnp.float32
BF16 = jnp.bfloat16


def _rms(x, w):
    var = jnp.mean(x * x, axis=-1, keepdims=True)
    return x * lax.rsqrt(var + NORM_EPS) * w


def _dot(a, b):
    return jnp.dot(a, b, preferred_element_type=F32)


def _dot_nt(a, b):
    return lax.dot_general(a, b, (((1,), (1,)), ((), ())), preferred_element_type=F32)


def _moe_combine(x, ya_ref, yb_ref, rt_ref):
    rt = rt_ref[...]
    return x + rt[:, TOP_K:TOP_K + 1] * ya_ref[...] + rt[:, TOP_K + 1:TOP_K + 2] * yb_ref[...]


def _mixer_kernel(ts, moe_in, router, *refs):
    refs = list(refs)
    x_ref = refs.pop(0)
    ya_ref = refs.pop(0) if moe_in else None
    yb_ref = refs.pop(0) if moe_in else None
    rtp_ref = refs.pop(0) if moe_in else None
    (cos_ref, sin_ref, anw_ref, win_ref, sinks_ref, convw_ref, hnw_ref, wout_ref, fnw_ref,
     gsum_ref) = refs[:10]
    refs = refs[10:]
    rw_ref = refs.pop(0) if router else None
    xo_ref = refs.pop(0)
    h2_ref = refs.pop(0)
    rt_ref = refs.pop(0) if router else None
    q_s, k_s, v_s, u_s, y_s, c_s, h_s = refs

    t = pl.program_id(1)
    nblk = ts // WINDOW

    @pl.when(t == 0)
    def _():
        k_s[:, 0:WINDOW, :] = jnp.zeros((4, WINDOW, LANES), BF16)
        v_s[:, 0:WINDOW, :] = jnp.zeros((4, WINDOW, LANES), BF16)
        u_s[0:SUBLANES, :] = jnp.zeros((SUBLANES, CONV_WIDTH), F32)

    @pl.when(t > 0)
    def _():
        k_s[:, 0:WINDOW, :] = k_s[:, ts:ts + WINDOW, :]
        v_s[:, 0:WINDOW, :] = v_s[:, ts:ts + WINDOW, :]
        u_s[0:SUBLANES, :] = u_s[ts:ts + SUBLANES, :]

    hr = ts // 2
    halves = [slice(0, hr), slice(hr, ts)]
    lane = lax.broadcasted_iota(jnp.int32, (hr, LANES), 1)
    first_half = (lane & (HEAD_DIM // 2)) == 0
    low_head = lane < HEAD_DIM
    scale = HEAD_DIM ** -0.5

    def qkv_proj(r):
        xh = x_ref[r, :]
        if moe_in:
            xh = _moe_combine(xh, ya_ref.at[r, :], yb_ref.at[r, :], rtp_ref.at[r, :])
            xo_ref[r, :] = xh
        hh = _rms(xh, anw_ref[...]).astype(BF16)
        h_s[r, :] = hh
        return _dot(hh, win_ref[:, 0:V_END])

    def rope_and_stage(r, proj):
        cosf = cos_ref[r, :]
        sinf = sin_ref[r, :]

        def rope(tq):
            rot = jnp.where(first_half, pltpu.roll(tq, LANES - HEAD_DIM // 2, 1), pltpu.roll(tq, HEAD_DIM // 2, 1))
            return tq * cosf + rot * sinf

        for p in range(ATTN_WIDTH // LANES):
            q_s[r, p * LANES:(p + 1) * LANES] = (rope(proj[:, p * LANES:(p + 1) * LANES]) * scale).astype(BF16)

        def lane_variants(a, dst):
            ar = pltpu.roll(a, HEAD_DIM, 1)
            zero = jnp.zeros_like(a)
            rows = slice(WINDOW + r.start, WINDOW + r.stop)
            dst[0, rows, :] = jnp.where(low_head, a, zero).astype(BF16)
            dst[1, rows, :] = jnp.where(low_head, zero, ar).astype(BF16)
            dst[2, rows, :] = jnp.where(low_head, ar, zero).astype(BF16)
            dst[3, rows, :] = jnp.where(low_head, zero, a).astype(BF16)

        lane_variants(rope(proj[:, Q_END:K_END]), k_s)
        lane_variants(proj[:, K_END:V_END], v_s)

    projs = [qkv_proj(r) for r in halves]
    for r, proj in zip(halves, projs):
        rope_and_stage(r, proj)

    qi = lax.broadcasted_iota(jnp.int32, (WINDOW, 2 * WINDOW), 0)
    kj = lax.broadcasted_iota(jnp.int32, (WINDOW, 2 * WINDOW), 1)
    band = (kj > qi) & (kj <= qi + WINDOW)
    band_first = band & (kj >= jnp.where(t > 0, 0, WINDOW))

    for b in range(nblk):
        mask = band_first if b == 0 else band
        rows = slice(b * WINDOW, (b + 1) * WINDOW)
        ctx = slice(b * WINDOW, (b + 2) * WINDOW)
        for g in range(N_KV_HEADS):
            cchunk = b * N_KV_HEADS + g
            if cchunk < 3 * CONV_WIDTH // MXU_DIM:
                ccols = slice(cchunk * MXU_DIM, (cchunk + 1) * MXU_DIM)
                c_s[:, ccols] = _dot(h_s[...], win_ref[:, V_END + cchunk * MXU_DIM:V_END + (cchunk + 1) * MXU_DIM])
            qg = jnp.concatenate([q_s[rows, (2 * g) * LANES:(2 * g + 1) * LANES],
                                  q_s[rows, (2 * g + 1) * LANES:(2 * g + 2) * LANES]], axis=0)
            probs = []
            for j in range(2):
                s = _dot_nt(qg, k_s[2 * g + j, ctx, :])
                head_probs = []
                for r in range(2):
                    sink = sinks_ref[4 * g + 2 * r + j]
                    sr = jnp.where(mask, s[r * WINDOW:(r + 1) * WINDOW], NEG_INF)
                    m = jnp.maximum(jnp.max(sr, axis=-1, keepdims=True), sink)
                    e = jnp.exp(sr - m)
                    denom = jnp.sum(e, axis=-1, keepdims=True) + jnp.exp(sink - m)
                    head_probs.append((e * (1.0 / denom)).astype(BF16))
                probs.append(jnp.concatenate(head_probs, axis=0))
            o = _dot(probs[0], v_s[2 * g, ctx, :]) + _dot(probs[1], v_s[2 * g + 1, ctx, :])
            y_s[rows, (2 * g) * LANES:(2 * g + 1) * LANES] = o[0:WINDOW]
            y_s[rows, (2 * g + 1) * LANES:(2 * g + 2) * LANES] = o[WINDOW:2 * WINDOW]

    for cchunk in range(nblk * N_KV_HEADS, 3 * CONV_WIDTH // MXU_DIM):
        c_s[:, cchunk * MXU_DIM:(cchunk + 1) * MXU_DIM] = _dot(
            h_s[...], win_ref[:, V_END + cchunk * MXU_DIM:V_END + (cchunk + 1) * MXU_DIM])

    u_s[SUBLANES:SUBLANES + ts, :] = c_s[:, CONV_WIDTH:2 * CONV_WIDTH] * c_s[:, 2 * CONV_WIDTH:3 * CONV_WIDTH]
    cw = convw_ref[...]

    def mix_out(r):
        conv = cw[0:1, :] * u_s[SUBLANES - 2 + r.start:SUBLANES - 2 + r.stop, :]
        conv = conv + cw[1:2, :] * u_s[SUBLANES - 1 + r.start:SUBLANES - 1 + r.stop, :]
        conv = conv + cw[2:3, :] * u_s[SUBLANES + r.start:SUBLANES + r.stop, :]
        y = jnp.concatenate([y_s[r, :], c_s[r, 0:CONV_WIDTH] * conv], axis=-1)
        sq = (y * y).astype(BF16)
        ms = jnp.concatenate([_dot(sq[:, c * MXU_DIM:(c + 1) * MXU_DIM], gsum_ref[...])
                              for c in range(D_MODEL // MXU_DIM)], axis=-1)
        yn = (y * lax.rsqrt(ms + NORM_EPS) * hnw_ref[...]).astype(BF16)
        return _dot(yn, wout_ref[...])

    def residual(r, out):
        xn = (xo_ref[r, :] if moe_in else x_ref[r, :]) + out
        xo_ref[r, :] = xn
        h2 = _rms(xn, fnw_ref[...])
        h2_ref[r, :] = h2.astype(h2_ref.dtype)
        return _dot(h2.astype(BF16), rw_ref[...]) if router else None

    def top2(r, lg):
        lanef = lane.astype(F32)
        l1 = jnp.where(lane < N_EXPERTS, lg, -jnp.inf)
        m1 = jnp.max(l1, axis=-1, keepdims=True)
        i1 = jnp.min(jnp.where(l1 == m1, lanef, float(LANES)), axis=-1, keepdims=True)
        l2 = jnp.where(lanef == i1, -jnp.inf, l1)
        m2 = jnp.max(l2, axis=-1, keepdims=True)
        i2 = jnp.min(jnp.where(l2 == m2, lanef, float(LANES)), axis=-1, keepdims=True)
        e2 = jnp.exp(m2 - m1)
        inv = 1.0 / (1.0 + e2)
        out = jnp.where(lane == 0, i1, jnp.where(lane == 1, i2, jnp.where(lane == 2, inv, e2 * inv)))
        rt_ref[r, :] = jnp.where(lane < 4, out, 0.0)

    outs = [mix_out(r) for r in halves]
    logits = [residual(r, out) for r, out in zip(halves, outs)]
    if router:
        for r, lg in zip(halves, logits):
            top2(r, lg)


def _resident(shape, layer=None):
    if layer is None:
        return pl.BlockSpec(shape, lambda *_: (0,) * len(shape), pipeline_mode=pl.Buffered(1))
    return pl.BlockSpec((None,) + shape, lambda *_: (layer,) + (0,) * len(shape), pipeline_mode=pl.Buffered(1))


def _mixer(x, moe, cosf, sinf, anw, win, sinks, convw, hnw, wout, fnw, gsum, rw, layer, seq, h2_dtype):
    n = x.shape[0]
    ts = min(SEQ_TILE, seq)
    nt = seq // ts
    nb = n // seq
    moe_in = moe is not None
    router = rw is not None
    row = lambda b, t: (b * nt + t, 0)
    tile = pl.BlockSpec((ts, D_MODEL), row)
    lanes_tile = pl.BlockSpec((ts, LANES), row)
    in_specs = [tile]
    args = [x]
    if moe_in:
        ymoe, rt_prev = moe
        off = n // ts
        in_specs += [tile, pl.BlockSpec((ts, D_MODEL), lambda b, t: (off + b * nt + t, 0)), lanes_tile]
        args += [ymoe, ymoe, rt_prev]
    in_specs += [lanes_tile, lanes_tile,
                 _resident((1, D_MODEL), layer), _resident((D_MODEL, IN_WIDTH), layer),
                 pl.BlockSpec(memory_space=pltpu.SMEM),
                 _resident((CONV_K, CONV_WIDTH), layer), _resident((1, D_MODEL), layer),
                 _resident((D_MODEL, D_MODEL), layer), _resident((1, D_MODEL), layer),
                 _resident((MXU_DIM, MXU_DIM))]
    args += [cosf, sinf, anw, win, sinks, convw, hnw, wout, fnw, gsum]
    out_shape = [jax.ShapeDtypeStruct((n, D_MODEL), F32), jax.ShapeDtypeStruct((n, D_MODEL), h2_dtype)]
    out_specs = [tile, tile]
    if router:
        in_specs.append(_resident((D_MODEL, LANES)))
        args.append(rw)
        out_shape.append(jax.ShapeDtypeStruct((n, LANES), F32))
        out_specs.append(lanes_tile)
    return pl.pallas_call(
        functools.partial(_mixer_kernel, ts, moe_in, router),
        grid=(nb, nt),
        in_specs=in_specs,
        out_specs=out_specs,
        out_shape=out_shape,
        scratch_shapes=[pltpu.VMEM((ts, ATTN_WIDTH), BF16),
                        pltpu.VMEM((4, WINDOW + ts, LANES), BF16),
                        pltpu.VMEM((4, WINDOW + ts, LANES), BF16),
                        pltpu.VMEM((SUBLANES + ts, CONV_WIDTH), F32),
                        pltpu.VMEM((ts, ATTN_WIDTH), F32),
                        pltpu.VMEM((ts, 3 * CONV_WIDTH), F32),
                        pltpu.VMEM((ts, D_MODEL), BF16)],
        compiler_params=pltpu.CompilerParams(dimension_semantics=("arbitrary", "arbitrary"),
                                             vmem_limit_bytes=VMEM_LIMIT_BYTES),
        name="mixer",
    )(*args)


def _swiglu(hb, wg_ref, wu_ref, wd_ref, acc_ref, side_work=None):
    for c in range(D_FF // FF_CHUNK):
        if side_work is not None:
            side_work(c)
        cols = slice(c * FF_CHUNK, (c + 1) * FF_CHUNK)
        g = _dot(hb, wg_ref[:, cols])
        u = _dot(hb, wu_ref[:, cols])
        a = (jax.nn.silu(g) * u).astype(BF16)
        part = _dot(a, wd_ref[cols, :])
        if c == 0:
            acc_ref[...] = part
        elif c < D_FF // FF_CHUNK - 1:
            acc_ref[...] += part
    return acc_ref[...] + part


def _ffn_kernel(n_cast, x_ref, h_ref, wg_ref, wu_ref, wd_ref, *refs):
    cast_in, o_ref, cast_out, acc_ref = refs[:n_cast], refs[n_cast], refs[n_cast + 1:-1], refs[-1]
    o_ref[...] = x_ref[...] + _swiglu(h_ref[...], wg_ref, wu_ref, wd_ref, acc_ref)
    for src, dst in zip(cast_in, cast_out):
        dst[...] = src[...].astype(dst.dtype)


def _dense_ffn(x, h2, wg, wu, wd, layer, cast=None):
    n = x.shape[0]
    tm = min(FFN_TILE, n)
    steps = n // tm
    tile = pl.BlockSpec((tm, D_MODEL), lambda i: (i, 0))
    in_specs = [tile, tile, _resident((D_MODEL, D_FF), layer), _resident((D_MODEL, D_FF), layer),
                _resident((D_FF, D_MODEL), layer)]
    out_specs = [tile]
    out_shape = [jax.ShapeDtypeStruct((n, D_MODEL), F32)]
    args = [x, h2, wg, wu, wd]
    if cast is not None:
        weights, cast_layer = cast
        per_expert = steps // N_EXPERTS
        assert per_expert * N_EXPERTS == steps
        for w in weights:
            rows, cols = w.shape[2] // per_expert, w.shape[3]
            in_specs.append(pl.BlockSpec((None, None, rows, cols),
                                         lambda i: (cast_layer, i // per_expert, i % per_expert, 0)))
            out_specs.append(pl.BlockSpec((None, rows, cols), lambda i: (i // per_expert, i % per_expert, 0)))
            out_shape.append(jax.ShapeDtypeStruct(w.shape[1:], BF16))
            args.append(w)
    outs = pl.pallas_call(
        functools.partial(_ffn_kernel, len(args) - 5),
        grid=(steps,),
        in_specs=in_specs,
        out_specs=out_specs,
        out_shape=out_shape,
        scratch_shapes=[pltpu.VMEM((tm, D_MODEL), F32)],
        compiler_params=pltpu.CompilerParams(dimension_semantics=("arbitrary",),
                                             vmem_limit_bytes=VMEM_LIMIT_BYTES),
        name="dense_ffn",
    )(*args)
    return outs[0], outs[1:]


def _moe_kernel(blk_e_ref, src0_ref, srcn_ref, dstp_ref, dstl_ref, h_hbm, wg_ref, wu_ref, wd_ref, y_hbm,
                xbuf, ybuf, acc_ref, gsem, ssem):
    i = pl.program_id(0)
    n_real = blk_e_ref[pl.num_programs(0)]
    slot = i % 2
    other = 1 - slot
    bm = MOE_BLOCK
    per_chunk = bm // MOE_ISSUE_CHUNKS

    def gather_row(idx_ref, r, block_slot):
        pltpu.make_async_copy(h_hbm.at[pl.ds(idx_ref[0, 0, r], 1)], xbuf.at[block_slot, pl.ds(r, 1)],
                              gsem.at[block_slot]).start(priority=0)

    def scatter_row(idx_ref, r, block_slot):
        pltpu.make_async_copy(ybuf.at[block_slot, pl.ds(r, 1)], y_hbm.at[pl.ds(idx_ref[0, 0, r], 1)],
                              ssem.at[block_slot]).start(priority=1)

    def wait_gather(block_slot):
        pltpu.make_async_copy(h_hbm.at[pl.ds(0, bm)], xbuf.at[block_slot], gsem.at[block_slot]).wait()

    def wait_scatter(block_slot):
        pltpu.make_async_copy(ybuf.at[block_slot], y_hbm.at[pl.ds(0, bm)], ssem.at[block_slot]).wait()

    @pl.when(i == 0)
    def _():
        def body(r, c):
            gather_row(src0_ref, r, 0)
            return c
        lax.fori_loop(0, bm, body, 0)
        ybuf[...] = jnp.zeros((2, bm, D_MODEL), F32)
        pltpu.make_async_copy(ybuf.at[0], y_hbm.at[pl.ds(y_hbm.shape[0] - bm, bm)], ssem.at[0]).start()

    @pl.when(i < n_real)
    def _():
        wait_gather(slot)

        def side_work(c):
            if c < MOE_ISSUE_CHUNKS:
                for r in range(c * per_chunk, (c + 1) * per_chunk):
                    gather_row(srcn_ref, r, other)
                    scatter_row(dstp_ref, r, other)

        total = _swiglu(xbuf[slot].astype(BF16), wg_ref, wu_ref, wd_ref, acc_ref, side_work)
        wait_scatter(slot)
        ybuf[slot] = total

    @pl.when(i >= n_real)
    def _():
        ybuf[0] = jnp.zeros((bm, D_MODEL), F32)
        fill = pltpu.make_async_copy(ybuf.at[0], y_hbm.at[pl.ds(pl.multiple_of(i * bm, bm), bm)], ssem.at[0])
        fill.start()
        fill.wait()

    @pl.when(i == n_real - 1)
    def _():
        def body(r, c):
            scatter_row(dstl_ref, r, slot)
            return c
        lax.fori_loop(0, bm, body, 0)
        wait_scatter(slot)
        wait_scatter(other)
        wait_gather(other)


def _moe_experts(h2, blk_e, src, dst, wg, wu, wd):
    nblocks = src.shape[0]
    bm = MOE_BLOCK
    n_rows = (nblocks + 2) * bm
    wspec_in = pl.BlockSpec((None, D_MODEL, D_FF), lambda i, be: (be[i], 0, 0), pipeline_mode=pl.Buffered(1))
    wspec_out = pl.BlockSpec((None, D_FF, D_MODEL), lambda i, be: (be[i], 0, 0), pipeline_mode=pl.Buffered(1))
    smem_rows = lambda index_map: pl.BlockSpec((1, 1, bm), index_map, memory_space=pltpu.SMEM)
    grid_spec = pltpu.PrefetchScalarGridSpec(
        num_scalar_prefetch=1,
        grid=(nblocks,),
        in_specs=[smem_rows(lambda i, be: (0, 0, 0)),
                  smem_rows(lambda i, be: (jnp.minimum(i + 1, nblocks - 1), 0, 0)),
                  smem_rows(lambda i, be: (i, 0, 0)),
                  smem_rows(lambda i, be: (be[nblocks], 0, 0)),
                  pl.BlockSpec(memory_space=pl.ANY),
                  wspec_in, wspec_in, wspec_out],
        out_specs=pl.BlockSpec(memory_space=pl.ANY),
        scratch_shapes=[pltpu.VMEM((2, bm, D_MODEL), F32),
                        pltpu.VMEM((2, bm, D_MODEL), F32),
                        pltpu.VMEM((bm, D_MODEL), F32),
                        pltpu.SemaphoreType.DMA((2,)),
                        pltpu.SemaphoreType.DMA((2,))],
    )
    return pl.pallas_call(
        _moe_kernel,
        grid_spec=grid_spec,
        out_shape=jax.ShapeDtypeStruct((n_rows, D_MODEL), F32),
        compiler_params=pltpu.CompilerParams(dimension_semantics=("arbitrary",),
                                             vmem_limit_bytes=VMEM_LIMIT_BYTES),
        name="moe_experts",
    )(blk_e, src, src, dst, dst, h2, wg, wu, wd)


def _route(rt, n):
    bm = MOE_BLOCK
    n_assign = n * TOP_K
    n_rows = n_assign + N_EXPERTS * bm
    nblocks = n_rows // bm
    flat_e = rt[:, 0:TOP_K].astype(jnp.int32).reshape(-1)
    order = jnp.argsort(flat_e).astype(jnp.int32)
    experts = jnp.arange(N_EXPERTS, dtype=jnp.int32)
    sizes = jnp.sum((flat_e[:, None] == experts[None, :]).astype(jnp.int32), axis=0)
    padded = (sizes + bm - 1) // bm * bm
    group_end = jnp.cumsum(sizes)
    padded_end = jnp.cumsum(padded)
    blk_start = jnp.arange(nblocks, dtype=jnp.int32) * bm
    blk_e = jnp.minimum(jnp.sum((blk_start[:, None] >= padded_end[None, :]).astype(jnp.int32), axis=1),
                        N_EXPERTS - 1)
    rank = blk_start[:, None] + jnp.arange(bm, dtype=jnp.int32)[None, :] - (padded_end - padded)[blk_e][:, None]
    valid = rank < sizes[blk_e][:, None]
    sorted_pos = (group_end - sizes)[blk_e][:, None] + rank
    assign = order[jnp.clip(sorted_pos, 0, n_assign - 1)]
    tok = assign // TOP_K
    src = jnp.where(valid, tok, 0)
    slot = blk_start[:, None] + jnp.arange(bm, dtype=jnp.int32)[None, :]
    dst = jnp.where(valid, (assign % TOP_K) * n + tok, n_assign + slot - group_end[blk_e][:, None])
    spare = n_rows + jnp.arange(bm, dtype=jnp.int32)
    dst = jnp.concatenate([spare[None, :], dst], axis=0)
    n_real = padded_end[-1:] // bm
    return jnp.concatenate([blk_e, n_real]), src.reshape(nblocks, 1, bm), dst.reshape(nblocks + 1, 1, bm)


def _final_kernel(moe_in, *refs):
    if moe_in:
        x_ref, ya_ref, yb_ref, rt_ref, w_ref, o_ref = refs
        x = _moe_combine(x_ref[...], ya_ref, yb_ref, rt_ref)
    else:
        x_ref, w_ref, o_ref = refs
        x = x_ref[...]
    o_ref[...] = _rms(x, w_ref[...])


def _final(x, moe, w):
    n = x.shape[0]
    tm = min(1024, n)
    off = n // tm
    tile = pl.BlockSpec((tm, D_MODEL), lambda i: (i, 0))
    in_specs = [tile]
    args = [x]
    if moe is not None:
        in_specs += [tile, pl.BlockSpec((tm, D_MODEL), lambda i: (off + i, 0)),
                     pl.BlockSpec((tm, LANES), lambda i: (i, 0))]
        args += [moe[0], moe[0], moe[1]]
    return pl.pallas_call(
        functools.partial(_final_kernel, moe is not None),
        grid=(n // tm,),
        in_specs=in_specs + [_resident((1, D_MODEL))],
        out_specs=tile,
        out_shape=jax.ShapeDtypeStruct((n, D_MODEL), F32),
        compiler_params=pltpu.CompilerParams(dimension_semantics=("arbitrary",)),
        name="final_norm",
    )(*args, w)


def kernel(x, positions, attn_norm_w, w_in, attn_sinks, conv_w, head_norm_w, w_out, ffn_norm_w, dense_w_gate,
           dense_w_up, dense_w_down, router_w, moe_w_gate, moe_w_up, moe_w_down, final_norm_w):
    bsz, seq, _ = x.shape
    n = bsz * seq
    depth = w_in.shape[0]

    inv_freq = ROPE_THETA ** (-jnp.arange(0, HEAD_DIM, 2, dtype=F32) / HEAD_DIM)
    ang = positions.astype(F32)[..., None] * inv_freq
    cos, sin = jnp.cos(ang), jnp.sin(ang)
    cosf = jnp.tile(cos, (1, 1, 4)).reshape(n, LANES)
    sinf = jnp.tile(jnp.concatenate([-sin, sin], axis=-1), (1, 1, 2)).reshape(n, LANES)

    grp = jnp.arange(MXU_DIM) // HEAD_DIM
    gsum = jnp.where(grp[:, None] == grp[None, :], 1.0 / HEAD_DIM, 0.0).astype(BF16)

    win_b, wout_b = w_in.astype(BF16), w_out.astype(BF16)
    dense_b = [w.astype(BF16) for w in (dense_w_gate, dense_w_up, dense_w_down)]
    moe_f = (moe_w_gate, moe_w_up, moe_w_down)
    rw_b = jnp.pad(router_w, ((0, 0), (0, 0), (0, LANES - N_EXPERTS))).astype(BF16)
    anw, hnw, fnw = (w[:, None, :] for w in (attn_norm_w, head_norm_w, ffn_norm_w))

    cast_in_dense = (n // min(FFN_TILE, n)) % N_EXPERTS == 0

    xf = x.reshape(n, D_MODEL)
    moe = None
    moe_b = ()
    for l in range(depth):
        is_moe = l % 2 == 1
        i = l // 2
        outs = _mixer(xf, moe, cosf, sinf, anw, win_b, attn_sinks[l], conv_w, hnw, wout_b, fnw, gsum,
                      rw_b[i] if is_moe else None, l, seq, F32 if is_moe else BF16)
        if is_moe:
            xf, h2, rt = outs
            blk_e, src, dst = _route(rt, n)
            if not moe_b:
                moe_b = [w[i].astype(BF16) for w in moe_f]
            moe = (_moe_experts(h2, blk_e, src, dst, *moe_b), rt)
            moe_b = ()
        else:
            xf, h2 = outs
            cast = (moe_f, i) if cast_in_dense and l + 1 < depth else None
            xf, moe_b = _dense_ffn(xf, h2, *dense_b, i, cast)
            moe = None
    out = _final(xf, moe, final_norm_w[None])
    return out.reshape(bsz, seq, D_MODEL)
```

```python
import functools

import jax
import jax.numpy as jnp
from jax import lax
from jax.experimental import pallas as pl
from jax.experimental.pallas import tpu as pltpu

D_MODEL = 1024
HEAD_DIM = 64
N_Q_HEADS = 8
N_KV_HEADS = 2
ATTN_WIDTH = N_Q_HEADS * HEAD_DIM
KV_WIDTH = N_KV_HEADS * HEAD_DIM
CONV_WIDTH = D_MODEL - ATTN_WIDTH
Q_END = ATTN_WIDTH
K_END = Q_END + KV_WIDTH
V_END = K_END + KV_WIDTH
CB_END = V_END + CONV_WIDTH
CC_END = CB_END + CONV_WIDTH
IN_WIDTH = CC_END + CONV_WIDTH
WINDOW = 128
ROPE_THETA = 10000.0
CONV_K = 3
D_FF = 3584
N_EXPERTS = 8
TOP_K = 2
NORM_EPS = 1e-5
NEG_INF = -1e30

LANES = 128
SUBLANES = 8
MXU_DIM = 256
VMEM_LIMIT_BYTES = 58 * 1024 * 1024

SEQ_TILE = 512
FFN_TILE = 512
FF_CHUNK = 1792
MOE_BLOCK = 512
MOE_ISSUE_CHUNKS = min(4, D_FF // FF_CHUNK)

F32 = jnp.float32
BF16 = jnp.bfloat16


def _rms(x, w):
    var = jnp.mean(x * x, axis=-1, keepdims=True)
    return x * lax.rsqrt(var + NORM_EPS) * w


def _dot(a, b):
    return jnp.dot(a, b, preferred_element_type=F32)


def _dot_nt(a, b):
    return lax.dot_general(a, b, (((1,), (1,)), ((), ())), preferred_element_type=F32)


def _moe_combine(x, ya_ref, yb_ref, rt_ref):
    rt = rt_ref[...]
    return x + rt[:, TOP_K:TOP_K + 1] * ya_ref[...] + rt[:, TOP_K + 1:TOP_K + 2] * yb_ref[...]


def _mixer_kernel(ts, moe_in, router, *refs):
    refs = list(refs)
    x_ref = refs.pop(0)
    ya_ref = refs.pop(0) if moe_in else None
    yb_ref = refs.pop(0) if moe_in else None
    rtp_ref = refs.pop(0) if moe_in else None
    (cos_ref, sin_ref, anw_ref, win_ref, sinks_ref, convw_ref, hnw_ref, wout_ref, fnw_ref,
     gsum_ref) = refs[:10]
    refs = refs[10:]
    rw_ref = refs.pop(0) if router else None
    xo_ref = refs.pop(0)
    h2_ref = refs.pop(0)
    rt_ref = refs.pop(0) if router else None
    q_s, k_s, v_s, u_s, y_s, c_s, h_s = refs

    t = pl.program_id(1)
    nblk = ts // WINDOW

    @pl.when(t == 0)
    def _():
        k_s[:, 0:WINDOW, :] = jnp.zeros((4, WINDOW, LANES), BF16)
        v_s[:, 0:WINDOW, :] = jnp.zeros((4, WINDOW, LANES), BF16)
        u_s[0:SUBLANES, :] = jnp.zeros((SUBLANES, CONV_WIDTH), F32)

    @pl.when(t > 0)
    def _():
        k_s[:, 0:WINDOW, :] = k_s[:, ts:ts + WINDOW, :]
        v_s[:, 0:WINDOW, :] = v_s[:, ts:ts + WINDOW, :]
        u_s[0:SUBLANES, :] = u_s[ts:ts + SUBLANES, :]

    hr = ts // 2
    halves = [slice(0, hr), slice(hr, ts)]
    lane = lax.broadcasted_iota(jnp.int32, (hr, LANES), 1)
    first_half = (lane & (HEAD_DIM // 2)) == 0
    low_head = lane < HEAD_DIM
    scale = HEAD_DIM ** -0.5

    def qkv_proj(r):
        xh = x_ref[r, :]
        if moe_in:
            xh = _moe_combine(xh, ya_ref.at[r, :], yb_ref.at[r, :], rtp_ref.at[r, :])
            xo_ref[r, :] = xh
        hh = _rms(xh, anw_ref[...]).astype(BF16)
        h_s[r, :] = hh
        return _dot(hh, win_ref[:, 0:V_END])

    def rope_and_stage(r, proj):
        cosf = jnp.concatenate([cos_ref[r, :]] * (2 * LANES // HEAD_DIM), axis=1)
        sin4 = jnp.concatenate([sin_ref[r, :]] * (2 * LANES // HEAD_DIM), axis=1)
        sinf = jnp.where(first_half, -sin4, sin4)

        def rope(tq):
            rot = jnp.where(first_half, pltpu.roll(tq, LANES - HEAD_DIM // 2, 1), pltpu.roll(tq, HEAD_DIM // 2, 1))
            return tq * cosf + rot * sinf

        for p in range(ATTN_WIDTH // LANES):
            q_s[r, p * LANES:(p + 1) * LANES] = (rope(proj[:, p * LANES:(p + 1) * LANES]) * scale).astype(BF16)

        def lane_variants(a, dst):
            ar = pltpu.roll(a, HEAD_DIM, 1)
            zero = jnp.zeros_like(a)
            rows = slice(WINDOW + r.start, WINDOW + r.stop)
            dst[0, rows, :] = jnp.where(low_head, a, zero).astype(BF16)
            dst[1, rows, :] = jnp.where(low_head, zero, ar).astype(BF16)
            dst[2, rows, :] = jnp.where(low_head, ar, zero).astype(BF16)
            dst[3, rows, :] = jnp.where(low_head, zero, a).astype(BF16)

        lane_variants(rope(proj[:, Q_END:K_END]), k_s)
        lane_variants(proj[:, K_END:V_END], v_s)

    projs = [qkv_proj(r) for r in halves]
    for r, proj in zip(halves, projs):
        rope_and_stage(r, proj)

    qi = lax.broadcasted_iota(jnp.int32, (WINDOW, 2 * WINDOW), 0)
    kj = lax.broadcasted_iota(jnp.int32, (WINDOW, 2 * WINDOW), 1)
    band = (kj > qi) & (kj <= qi + WINDOW)
    band_first = band & (kj >= jnp.where(t > 0, 0, WINDOW))

    for b in range(nblk):
        mask = band_first if b == 0 else band
        rows = slice(b * WINDOW, (b + 1) * WINDOW)
        ctx = slice(b * WINDOW, (b + 2) * WINDOW)
        for g in range(N_KV_HEADS):
            cchunk = b * N_KV_HEADS + g
            if cchunk < 3 * CONV_WIDTH // MXU_DIM:
                ccols = slice(cchunk * MXU_DIM, (cchunk + 1) * MXU_DIM)
                c_s[:, ccols] = _dot(h_s[...], win_ref[:, V_END + cchunk * MXU_DIM:V_END + (cchunk + 1) * MXU_DIM])
            qg = jnp.concatenate([q_s[rows, (2 * g) * LANES:(2 * g + 1) * LANES],
                                  q_s[rows, (2 * g + 1) * LANES:(2 * g + 2) * LANES]], axis=0)
            probs = []
            for j in range(2):
                s = _dot_nt(qg, k_s[2 * g + j, ctx, :])
                head_probs = []
                for r in range(2):
                    sink = sinks_ref[4 * g + 2 * r + j]
                    sr = jnp.where(mask, s[r * WINDOW:(r + 1) * WINDOW], NEG_INF)
                    m = jnp.maximum(jnp.max(sr, axis=-1, keepdims=True), sink)
                    e = jnp.exp(sr - m)
                    denom = jnp.sum(e, axis=-1, keepdims=True) + jnp.exp(sink - m)
                    head_probs.append((e * (1.0 / denom)).astype(BF16))
                probs.append(jnp.concatenate(head_probs, axis=0))
            o = _dot(probs[0], v_s[2 * g, ctx, :]) + _dot(probs[1], v_s[2 * g + 1, ctx, :])
            y_s[rows, (2 * g) * LANES:(2 * g + 1) * LANES] = o[0:WINDOW]
            y_s[rows, (2 * g + 1) * LANES:(2 * g + 2) * LANES] = o[WINDOW:2 * WINDOW]

    for cchunk in range(nblk * N_KV_HEADS, 3 * CONV_WIDTH // MXU_DIM):
        c_s[:, cchunk * MXU_DIM:(cchunk + 1) * MXU_DIM] = _dot(
            h_s[...], win_ref[:, V_END + cchunk * MXU_DIM:V_END + (cchunk + 1) * MXU_DIM])

    u_s[SUBLANES:SUBLANES + ts, :] = c_s[:, CONV_WIDTH:2 * CONV_WIDTH] * c_s[:, 2 * CONV_WIDTH:3 * CONV_WIDTH]
    cw = convw_ref[...]

    def mix_out(r):
        conv = cw[0:1, :] * u_s[SUBLANES - 2 + r.start:SUBLANES - 2 + r.stop, :]
        conv = conv + cw[1:2, :] * u_s[SUBLANES - 1 + r.start:SUBLANES - 1 + r.stop, :]
        conv = conv + cw[2:3, :] * u_s[SUBLANES + r.start:SUBLANES + r.stop, :]
        y = jnp.concatenate([y_s[r, :], c_s[r, 0:CONV_WIDTH] * conv], axis=-1)
        sq = (y * y).astype(BF16)
        ms = jnp.concatenate([_dot(sq[:, c * MXU_DIM:(c + 1) * MXU_DIM], gsum_ref[...])
                              for c in range(D_MODEL // MXU_DIM)], axis=-1)
        yn = (y * lax.rsqrt(ms + NORM_EPS) * hnw_ref[...]).astype(BF16)
        return _dot(yn, wout_ref[...])

    def residual(r, out):
        xn = (xo_ref[r, :] if moe_in else x_ref[r, :]) + out
        xo_ref[r, :] = xn
        h2 = _rms(xn, fnw_ref[...])
        h2_ref[r, :] = h2.astype(h2_ref.dtype)
        return _dot(h2.astype(BF16), rw_ref[...]) if router else None

    def top2(r, lg):
        lanef = lane.astype(F32)
        l1 = jnp.where(lane < N_EXPERTS, lg, -jnp.inf)
        m1 = jnp.max(l1, axis=-1, keepdims=True)
        i1 = jnp.min(jnp.where(l1 == m1, lanef, float(LANES)), axis=-1, keepdims=True)
        l2 = jnp.where(lanef == i1, -jnp.inf, l1)
        m2 = jnp.max(l2, axis=-1, keepdims=True)
        i2 = jnp.min(jnp.where(l2 == m2, lanef, float(LANES)), axis=-1, keepdims=True)
        e2 = jnp.exp(m2 - m1)
        inv = 1.0 / (1.0 + e2)
        out = jnp.where(lane == 0, i1, jnp.where(lane == 1, i2, jnp.where(lane == 2, inv, e2 * inv)))
        rt_ref[r, :] = jnp.where(lane < 4, out, 0.0)

    outs = [mix_out(r) for r in halves]
    logits = [residual(r, out) for r, out in zip(halves, outs)]
    if router:
        for r, lg in zip(halves, logits):
            top2(r, lg)


def _resident(shape, layer=None):
    if layer is None:
        return pl.BlockSpec(shape, lambda *_: (0,) * len(shape), pipeline_mode=pl.Buffered(1))
    return pl.BlockSpec((None,) + shape, lambda *_: (layer,) + (0,) * len(shape), pipeline_mode=pl.Buffered(1))


def _mixer(x, moe, cosf, sinf, anw, win, sinks, convw, hnw, wout, fnw, gsum, rw, layer, seq, h2_dtype):
    n = x.shape[0]
    ts = min(SEQ_TILE, seq)
    nt = seq // ts
    nb = n // seq
    moe_in = moe is not None
    router = rw is not None
    row = lambda b, t: (b * nt + t, 0)
    tile = pl.BlockSpec((ts, D_MODEL), row)
    lanes_tile = pl.BlockSpec((ts, LANES), row)
    in_specs = [tile]
    args = [x]
    if moe_in:
        ymoe, rt_prev = moe
        off = n // ts
        in_specs += [tile, pl.BlockSpec((ts, D_MODEL), lambda b, t: (off + b * nt + t, 0)), lanes_tile]
        args += [ymoe, ymoe, rt_prev]
    rope_tile = pl.BlockSpec((ts, HEAD_DIM // 2), row)
    in_specs += [rope_tile, rope_tile,
                 _resident((1, D_MODEL), layer), _resident((D_MODEL, IN_WIDTH), layer),
                 pl.BlockSpec(memory_space=pltpu.SMEM),
                 _resident((CONV_K, CONV_WIDTH), layer), _resident((1, D_MODEL), layer),
                 _resident((D_MODEL, D_MODEL), layer), _resident((1, D_MODEL), layer),
                 _resident((MXU_DIM, MXU_DIM))]
    args += [cosf, sinf, anw, win, sinks, convw, hnw, wout, fnw, gsum]
    out_shape = [jax.ShapeDtypeStruct((n, D_MODEL), F32), jax.ShapeDtypeStruct((n, D_MODEL), h2_dtype)]
    out_specs = [tile, tile]
    if router:
        in_specs.append(_resident((D_MODEL, LANES)))
        args.append(rw)
        out_shape.append(jax.ShapeDtypeStruct((n, LANES), F32))
        out_specs.append(lanes_tile)
    return pl.pallas_call(
        functools.partial(_mixer_kernel, ts, moe_in, router),
        grid=(nb, nt),
        in_specs=in_specs,
        out_specs=out_specs,
        out_shape=out_shape,
        scratch_shapes=[pltpu.VMEM((ts, ATTN_WIDTH), BF16),
                        pltpu.VMEM((4, WINDOW + ts, LANES), BF16),
                        pltpu.VMEM((4, WINDOW + ts, LANES), BF16),
                        pltpu.VMEM((SUBLANES + ts, CONV_WIDTH), F32),
                        pltpu.VMEM((ts, ATTN_WIDTH), F32),
                        pltpu.VMEM((ts, 3 * CONV_WIDTH), F32),
                        pltpu.VMEM((ts, D_MODEL), BF16)],
        compiler_params=pltpu.CompilerParams(dimension_semantics=("arbitrary", "arbitrary"),
                                             vmem_limit_bytes=VMEM_LIMIT_BYTES),
        name="mixer",
    )(*args)


def _swiglu(hb, wg_ref, wu_ref, wd_ref, acc_ref, side_work=None):
    for c in range(D_FF // FF_CHUNK):
        if side_work is not None:
            side_work(c)
        cols = slice(c * FF_CHUNK, (c + 1) * FF_CHUNK)
        g = _dot(hb, wg_ref[:, cols])
        u = _dot(hb, wu_ref[:, cols])
        a = (jax.nn.silu(g) * u).astype(BF16)
        part = _dot(a, wd_ref[cols, :])
        if c == 0:
            acc_ref[...] = part
        elif c < D_FF // FF_CHUNK - 1:
            acc_ref[...] += part
    return acc_ref[...] + part


def _ffn_kernel(n_cast, x_ref, h_ref, wg_ref, wu_ref, wd_ref, *refs):
    cast_in, o_ref, cast_out, acc_ref = refs[:n_cast], refs[n_cast], refs[n_cast + 1:-1], refs[-1]
    o_ref[...] = x_ref[...] + _swiglu(h_ref[...], wg_ref, wu_ref, wd_ref, acc_ref)
    for src, dst in zip(cast_in, cast_out):
        dst[...] = src[...].astype(dst.dtype)


def _dense_ffn(x, h2, wg, wu, wd, layer, cast=None):
    n = x.shape[0]
    tm = min(FFN_TILE, n)
    steps = n // tm
    tile = pl.BlockSpec((tm, D_MODEL), lambda i: (i, 0))
    in_specs = [tile, tile, _resident((D_MODEL, D_FF), layer), _resident((D_MODEL, D_FF), layer),
                _resident((D_FF, D_MODEL), layer)]
    out_specs = [tile]
    out_shape = [jax.ShapeDtypeStruct((n, D_MODEL), F32)]
    args = [x, h2, wg, wu, wd]
    if cast is not None:
        weights, cast_layer = cast
        per_expert = steps // N_EXPERTS
        assert per_expert * N_EXPERTS == steps
        for w in weights:
            rows, cols = w.shape[2] // per_expert, w.shape[3]
            in_specs.append(pl.BlockSpec((None, None, rows, cols),
                                         lambda i: (cast_layer, i // per_expert, i % per_expert, 0)))
            out_specs.append(pl.BlockSpec((None, rows, cols), lambda i: (i // per_expert, i % per_expert, 0)))
            out_shape.append(jax.ShapeDtypeStruct(w.shape[1:], BF16))
            args.append(w)
    outs = pl.pallas_call(
        functools.partial(_ffn_kernel, len(args) - 5),
        grid=(steps,),
        in_specs=in_specs,
        out_specs=out_specs,
        out_shape=out_shape,
        scratch_shapes=[pltpu.VMEM((tm, D_MODEL), F32)],
        compiler_params=pltpu.CompilerParams(dimension_semantics=("arbitrary",),
                                             vmem_limit_bytes=VMEM_LIMIT_BYTES),
        name="dense_ffn",
    )(*args)
    return outs[0], outs[1:]


def _moe_kernel(blk_e_ref, src0_ref, srcn_ref, dstp_ref, dstl_ref, h_hbm, wg_ref, wu_ref, wd_ref, y_hbm,
                xbuf, ybuf, acc_ref, gsem, ssem):
    i = pl.program_id(0)
    n_real = blk_e_ref[pl.num_programs(0)]
    slot = i % 2
    other = 1 - slot
    bm = MOE_BLOCK
    per_chunk = bm // MOE_ISSUE_CHUNKS

    def gather_row(idx_ref, r, block_slot):
        pltpu.make_async_copy(h_hbm.at[pl.ds(idx_ref[0, 0, r], 1)], xbuf.at[block_slot, pl.ds(r, 1)],
                              gsem.at[block_slot]).start(priority=0)

    def scatter_row(idx_ref, r, block_slot):
        pltpu.make_async_copy(ybuf.at[block_slot, pl.ds(r, 1)], y_hbm.at[pl.ds(idx_ref[0, 0, r], 1)],
                              ssem.at[block_slot]).start(priority=1)

    def wait_gather(block_slot):
        pltpu.make_async_copy(h_hbm.at[pl.ds(0, bm)], xbuf.at[block_slot], gsem.at[block_slot]).wait()

    def wait_scatter(block_slot):
        pltpu.make_async_copy(ybuf.at[block_slot], y_hbm.at[pl.ds(0, bm)], ssem.at[block_slot]).wait()

    @pl.when(i == 0)
    def _():
        def body(r, c):
            gather_row(src0_ref, r, 0)
            return c
        lax.fori_loop(0, bm, body, 0)
        ybuf[...] = jnp.zeros((2, bm, D_MODEL), F32)
        pltpu.make_async_copy(ybuf.at[0], y_hbm.at[pl.ds(y_hbm.shape[0] - bm, bm)], ssem.at[0]).start()

    @pl.when(i < n_real)
    def _():
        wait_gather(slot)

        def side_work(c):
            if c < MOE_ISSUE_CHUNKS:
                for r in range(c * per_chunk, (c + 1) * per_chunk):
                    gather_row(srcn_ref, r, other)
                    scatter_row(dstp_ref, r, other)

        total = _swiglu(xbuf[slot].astype(BF16), wg_ref, wu_ref, wd_ref, acc_ref, side_work)
        wait_scatter(slot)
        ybuf[slot] = total

    @pl.when(i >= n_real)
    def _():
        ybuf[0] = jnp.zeros((bm, D_MODEL), F32)
        fill = pltpu.make_async_copy(ybuf.at[0], y_hbm.at[pl.ds(pl.multiple_of(i * bm, bm), bm)], ssem.at[0])
        fill.start()
        fill.wait()

    @pl.when(i == n_real - 1)
    def _():
        def body(r, c):
            scatter_row(dstl_ref, r, slot)
            return c
        lax.fori_loop(0, bm, body, 0)
        wait_scatter(slot)
        wait_scatter(other)
        wait_gather(other)


def _moe_experts(h2, blk_e, src, dst, wg, wu, wd):
    nblocks = src.shape[0]
    bm = MOE_BLOCK
    n_rows = (nblocks + 2) * bm
    wspec_in = pl.BlockSpec((None, D_MODEL, D_FF), lambda i, be: (be[i], 0, 0), pipeline_mode=pl.Buffered(1))
    wspec_out = pl.BlockSpec((None, D_FF, D_MODEL), lambda i, be: (be[i], 0, 0), pipeline_mode=pl.Buffered(1))
    smem_rows = lambda index_map: pl.BlockSpec((1, 1, bm), index_map, memory_space=pltpu.SMEM)
    grid_spec = pltpu.PrefetchScalarGridSpec(
        num_scalar_prefetch=1,
        grid=(nblocks,),
        in_specs=[smem_rows(lambda i, be: (0, 0, 0)),
                  smem_rows(lambda i, be: (jnp.minimum(i + 1, nblocks - 1), 0, 0)),
                  smem_rows(lambda i, be: (i, 0, 0)),
                  smem_rows(lambda i, be: (be[nblocks], 0, 0)),
                  pl.BlockSpec(memory_space=pl.ANY),
                  wspec_in, wspec_in, wspec_out],
        out_specs=pl.BlockSpec(memory_space=pl.ANY),
        scratch_shapes=[pltpu.VMEM((2, bm, D_MODEL), F32),
                        pltpu.VMEM((2, bm, D_MODEL), F32),
                        pltpu.VMEM((bm, D_MODEL), F32),
                        pltpu.SemaphoreType.DMA((2,)),
                        pltpu.SemaphoreType.DMA((2,))],
    )
    return pl.pallas_call(
        _moe_kernel,
        grid_spec=grid_spec,
        out_shape=jax.ShapeDtypeStruct((n_rows, D_MODEL), F32),
        compiler_params=pltpu.CompilerParams(dimension_semantics=("arbitrary",),
                                             vmem_limit_bytes=VMEM_LIMIT_BYTES),
        name="moe_experts",
    )(blk_e, src, src, dst, dst, h2, wg, wu, wd)


def _route(rt, n):
    bm = MOE_BLOCK
    n_assign = n * TOP_K
    n_rows = n_assign + N_EXPERTS * bm
    nblocks = n_rows // bm
    flat_e = rt[:, 0:TOP_K].astype(jnp.int32).reshape(-1)
    order = jnp.argsort(flat_e).astype(jnp.int32)
    experts = jnp.arange(N_EXPERTS, dtype=jnp.int32)
    sizes = jnp.sum((flat_e[:, None] == experts[None, :]).astype(jnp.int32), axis=0)
    padded = (sizes + bm - 1) // bm * bm
    group_end = jnp.cumsum(sizes)
    padded_end = jnp.cumsum(padded)
    blk_start = jnp.arange(nblocks, dtype=jnp.int32) * bm
    blk_e = jnp.minimum(jnp.sum((blk_start[:, None] >= padded_end[None, :]).astype(jnp.int32), axis=1),
                        N_EXPERTS - 1)
    rank = blk_start[:, None] + jnp.arange(bm, dtype=jnp.int32)[None, :] - (padded_end - padded)[blk_e][:, None]
    valid = rank < sizes[blk_e][:, None]
    sorted_pos = (group_end - sizes)[blk_e][:, None] + rank
    assign = order[jnp.clip(sorted_pos, 0, n_assign - 1)]
    tok = assign // TOP_K
    src = jnp.where(valid, tok, 0)
    slot = blk_start[:, None] + jnp.arange(bm, dtype=jnp.int32)[None, :]
    dst = jnp.where(valid, (assign % TOP_K) * n + tok, n_assign + slot - group_end[blk_e][:, None])
    spare = n_rows + jnp.arange(bm, dtype=jnp.int32)
    dst = jnp.concatenate([spare[None, :], dst], axis=0)
    n_real = padded_end[-1:] // bm
    return jnp.concatenate([blk_e, n_real]), src.reshape(nblocks, 1, bm), dst.reshape(nblocks + 1, 1, bm)


def _final_kernel(moe_in, *refs):
    if moe_in:
        x_ref, ya_ref, yb_ref, rt_ref, w_ref, o_ref = refs
        x = _moe_combine(x_ref[...], ya_ref, yb_ref, rt_ref)
    else:
        x_ref, w_ref, o_ref = refs
        x = x_ref[...]
    o_ref[...] = _rms(x, w_ref[...])


def _final(x, moe, w):
    n = x.shape[0]
    tm = min(1024, n)
    off = n // tm
    tile = pl.BlockSpec((tm, D_MODEL), lambda i: (i, 0))
    in_specs = [tile]
    args = [x]
    if moe is not None:
        in_specs += [tile, pl.BlockSpec((tm, D_MODEL), lambda i: (off + i, 0)),
                     pl.BlockSpec((tm, LANES), lambda i: (i, 0))]
        args += [moe[0], moe[0], moe[1]]
    return pl.pallas_call(
        functools.partial(_final_kernel, moe is not None),
        grid=(n // tm,),
        in_specs=in_specs + [_resident((1, D_MODEL))],
        out_specs=tile,
        out_shape=jax.ShapeDtypeStruct((n, D_MODEL), F32),
        compiler_params=pltpu.CompilerParams(dimension_semantics=("arbitrary",)),
        name="final_norm",
    )(*args, w)


def kernel(x, positions, attn_norm_w, w_in, attn_sinks, conv_w, head_norm_w, w_out, ffn_norm_w, dense_w_gate,
           dense_w_up, dense_w_down, router_w, moe_w_gate, moe_w_up, moe_w_down, final_norm_w):
    bsz, seq, _ = x.shape
    n = bsz * seq
    depth = w_in.shape[0]

    inv_freq = ROPE_THETA ** (-jnp.arange(0, HEAD_DIM, 2, dtype=F32) / HEAD_DIM)
    ang = positions.astype(F32).reshape(n, 1) * inv_freq
    cosf, sinf = jnp.cos(ang), jnp.sin(ang)

    grp = jnp.arange(MXU_DIM) // HEAD_DIM
    gsum = jnp.where(grp[:, None] == grp[None, :], 1.0 / HEAD_DIM, 0.0).astype(BF16)

    win_b, wout_b = w_in.astype(BF16), w_out.astype(BF16)
    dense_b = [w.astype(BF16) for w in (dense_w_gate, dense_w_up, dense_w_down)]
    moe_f = (moe_w_gate, moe_w_up, moe_w_down)
    rw_b = jnp.pad(router_w, ((0, 0), (0, 0), (0, LANES - N_EXPERTS))).astype(BF16)
    anw, hnw, fnw = (w[:, None, :] for w in (attn_norm_w, head_norm_w, ffn_norm_w))

    cast_in_dense = (n // min(FFN_TILE, n)) % N_EXPERTS == 0

    xf = x.reshape(n, D_MODEL)
    moe = None
    moe_b = ()
    for l in range(depth):
        is_moe = l % 2 == 1
        i = l // 2
        outs = _mixer(xf, moe, cosf, sinf, anw, win_b, attn_sinks[l], conv_w, hnw, wout_b, fnw, gsum,
                      rw_b[i] if is_moe else None, l, seq, F32 if is_moe else BF16)
        if is_moe:
            xf, h2, rt = outs
            blk_e, src, dst = _route(rt, n)
            if not moe_b:
                moe_b = [w[i].astype(BF16) for w in moe_f]
            moe = (_moe_experts(h2, blk_e, src, dst, *moe_b), rt)
            moe_b = ()
        else:
            xf, h2 = outs
            cast = (moe_f, i) if cast_in_dense and l + 1 < depth else None
            xf, moe_b = _dense_ffn(xf, h2, *dense_b, i, cast)
            moe = None
    out = _final(xf, moe, final_norm_w[None])
    return out.reshape(bsz, seq, D_MODEL)
```

```python
import functools

import jax
import jax.numpy as jnp
from jax import lax
from jax.experimental import pallas as pl
from jax.experimental.pallas import tpu as pltpu

D_MODEL = 1024
HEAD_DIM = 64
N_Q_HEADS = 8
N_KV_HEADS = 2
ATTN_WIDTH = N_Q_HEADS * HEAD_DIM
KV_WIDTH = N_KV_HEADS * HEAD_DIM
CONV_WIDTH = D_MODEL - ATTN_WIDTH
Q_END = ATTN_WIDTH
K_END = Q_END + KV_WIDTH
V_END = K_END + KV_WIDTH
CB_END = V_END + CONV_WIDTH
CC_END = CB_END + CONV_WIDTH
IN_WIDTH = CC_END + CONV_WIDTH
WINDOW = 128
ROPE_THETA = 10000.0
CONV_K = 3
D_FF = 3584
N_EXPERTS = 8
TOP_K = 2
NORM_EPS = 1e-5
NEG_INF = -1e30

LANES = 128
SUBLANES = 8
MXU_DIM = 256
VMEM_LIMIT_BYTES = 58 * 1024 * 1024

SEQ_TILE = 512
FFN_TILE = 512
FF_CHUNK = 1792
MOE_BLOCK = 512
MOE_ISSUE_CHUNKS = min(4, D_FF // FF_CHUNK)

F32 = jnp.float32
BF16 = jnp.bfloat16


def _rms(x, w):
    var = jnp.mean(x * x, axis=-1, keepdims=True)
    return x * lax.rsqrt(var + NORM_EPS) * w


def _dot(a, b):
    return jnp.dot(a, b, preferred_element_type=F32)


def _dot_nt(a, b):
    return lax.dot_general(a, b, (((1,), (1,)), ((), ())), preferred_element_type=F32)


def _moe_combine(x, ya_ref, yb_ref, rt_ref):
    rt = rt_ref[...]
    return x + rt[:, TOP_K:TOP_K + 1] * ya_ref[...] + rt[:, TOP_K + 1:TOP_K + 2] * yb_ref[...]


def _mixer_kernel(ts, moe_in, router, *refs):
    refs = list(refs)
    x_ref = refs.pop(0)
    ya_ref = refs.pop(0) if moe_in else None
    yb_ref = refs.pop(0) if moe_in else None
    rtp_ref = refs.pop(0) if moe_in else None
    (cos_ref, sin_ref, anw_ref, win_ref, sinks_ref, convw_ref, hnw_ref, wout_ref, fnw_ref,
     gsum_ref) = refs[:10]
    refs = refs[10:]
    rw_ref = refs.pop(0) if router else None
    xo_ref = refs.pop(0)
    h2_ref = refs.pop(0)
    rt_ref = refs.pop(0) if router else None
    q_s, k_s, v_s, u_s, y_s, c_s, h_s = refs

    t = pl.program_id(1)
    nblk = ts // WINDOW

    @pl.when(t == 0)
    def _():
        k_s[:, 0:WINDOW, :] = jnp.zeros((4, WINDOW, LANES), BF16)
        v_s[:, 0:WINDOW, :] = jnp.zeros((4, WINDOW, LANES), BF16)
        u_s[0:SUBLANES, :] = jnp.zeros((SUBLANES, CONV_WIDTH), F32)

    @pl.when(t > 0)
    def _():
        k_s[:, 0:WINDOW, :] = k_s[:, ts:ts + WINDOW, :]
        v_s[:, 0:WINDOW, :] = v_s[:, ts:ts + WINDOW, :]
        u_s[0:SUBLANES, :] = u_s[ts:ts + SUBLANES, :]

    hr = ts // 2
    halves = [slice(0, hr), slice(hr, ts)]
    lane = lax.broadcasted_iota(jnp.int32, (hr, LANES), 1)
    first_half = (lane & (HEAD_DIM // 2)) == 0
    low_head = lane < HEAD_DIM
    scale = HEAD_DIM ** -0.5

    def qkv_proj(r):
        xh = x_ref[r, :]
        if moe_in:
            xh = _moe_combine(xh, ya_ref.at[r, :], yb_ref.at[r, :], rtp_ref.at[r, :])
            xo_ref[r, :] = xh
        hh = _rms(xh, anw_ref[...]).astype(BF16)
        h_s[r, :] = hh
        return _dot(hh, win_ref[:, 0:V_END])

    def rope_and_stage(r, proj):
        cosf = jnp.concatenate([cos_ref[r, :]] * (2 * LANES // HEAD_DIM), axis=1)
        sin4 = jnp.concatenate([sin_ref[r, :]] * (2 * LANES // HEAD_DIM), axis=1)
        sinf = jnp.where(first_half, -sin4, sin4)

        def rope(tq):
            rot = jnp.where(first_half, pltpu.roll(tq, LANES - HEAD_DIM // 2, 1), pltpu.roll(tq, HEAD_DIM // 2, 1))
            return tq * cosf + rot * sinf

        for p in range(ATTN_WIDTH // LANES):
            q_s[r, p * LANES:(p + 1) * LANES] = (rope(proj[:, p * LANES:(p + 1) * LANES]) * scale).astype(BF16)

        def lane_variants(a, dst):
            ar = pltpu.roll(a, HEAD_DIM, 1)
            zero = jnp.zeros_like(a)
            rows = slice(WINDOW + r.start, WINDOW + r.stop)
            dst[0, rows, :] = jnp.where(low_head, a, zero).astype(BF16)
            dst[1, rows, :] = jnp.where(low_head, zero, ar).astype(BF16)
            dst[2, rows, :] = jnp.where(low_head, ar, zero).astype(BF16)
            dst[3, rows, :] = jnp.where(low_head, zero, a).astype(BF16)

        lane_variants(rope(proj[:, Q_END:K_END]), k_s)
        lane_variants(proj[:, K_END:V_END], v_s)

    projs = [qkv_proj(r) for r in halves]
    for r, proj in zip(halves, projs):
        rope_and_stage(r, proj)

    qi = lax.broadcasted_iota(jnp.int32, (WINDOW, 2 * WINDOW), 0)
    kj = lax.broadcasted_iota(jnp.int32, (WINDOW, 2 * WINDOW), 1)
    band = (kj > qi) & (kj <= qi + WINDOW)
    band_first = band & (kj >= jnp.where(t > 0, 0, WINDOW))

    def scores(b, g):
        rows = slice(b * WINDOW, (b + 1) * WINDOW)
        ctx = slice(b * WINDOW, (b + 2) * WINDOW)
        qg = jnp.concatenate([q_s[rows, (2 * g) * LANES:(2 * g + 1) * LANES],
                              q_s[rows, (2 * g + 1) * LANES:(2 * g + 2) * LANES]], axis=0)
        return [_dot_nt(qg, k_s[2 * g + j, ctx, :]) for j in range(2)]

    steps = [(b, g) for b in range(nblk) for g in range(N_KV_HEADS)]
    s_next = scores(*steps[0])
    for idx, (b, g) in enumerate(steps):
        mask = band_first if b == 0 else band
        rows = slice(b * WINDOW, (b + 1) * WINDOW)
        ctx = slice(b * WINDOW, (b + 2) * WINDOW)
        s_cur = s_next
        if idx < 3 * CONV_WIDTH // MXU_DIM:
            ccols = slice(idx * MXU_DIM, (idx + 1) * MXU_DIM)
            c_s[:, ccols] = _dot(h_s[...], win_ref[:, V_END + idx * MXU_DIM:V_END + (idx + 1) * MXU_DIM])
        if idx + 1 < len(steps):
            s_next = scores(*steps[idx + 1])
        probs = []
        for j in range(2):
            head_probs = []
            for r in range(2):
                sink = sinks_ref[4 * g + 2 * r + j]
                sr = jnp.where(mask, s_cur[j][r * WINDOW:(r + 1) * WINDOW], NEG_INF)
                m = jnp.maximum(jnp.max(sr, axis=-1, keepdims=True), sink)
                e = jnp.exp(sr - m)
                denom = jnp.sum(e, axis=-1, keepdims=True) + jnp.exp(sink - m)
                head_probs.append((e * (1.0 / denom)).astype(BF16))
            probs.append(jnp.concatenate(head_probs, axis=0))
        o = _dot(probs[0], v_s[2 * g, ctx, :]) + _dot(probs[1], v_s[2 * g + 1, ctx, :])
        y_s[rows, (2 * g) * LANES:(2 * g + 1) * LANES] = o[0:WINDOW]
        y_s[rows, (2 * g + 1) * LANES:(2 * g + 2) * LANES] = o[WINDOW:2 * WINDOW]

    for cchunk in range(nblk * N_KV_HEADS, 3 * CONV_WIDTH // MXU_DIM):
        c_s[:, cchunk * MXU_DIM:(cchunk + 1) * MXU_DIM] = _dot(
            h_s[...], win_ref[:, V_END + cchunk * MXU_DIM:V_END + (cchunk + 1) * MXU_DIM])

    u_s[SUBLANES:SUBLANES + ts, :] = c_s[:, CONV_WIDTH:2 * CONV_WIDTH] * c_s[:, 2 * CONV_WIDTH:3 * CONV_WIDTH]
    cw = convw_ref[...]

    def mix_out(r):
        conv = cw[0:1, :] * u_s[SUBLANES - 2 + r.start:SUBLANES - 2 + r.stop, :]
        conv = conv + cw[1:2, :] * u_s[SUBLANES - 1 + r.start:SUBLANES - 1 + r.stop, :]
        conv = conv + cw[2:3, :] * u_s[SUBLANES + r.start:SUBLANES + r.stop, :]
        y = jnp.concatenate([y_s[r, :], c_s[r, 0:CONV_WIDTH] * conv], axis=-1)
        sq = (y * y).astype(BF16)
        ms = jnp.concatenate([_dot(sq[:, c * MXU_DIM:(c + 1) * MXU_DIM], gsum_ref[...])
                              for c in range(D_MODEL // MXU_DIM)], axis=-1)
        yn = (y * lax.rsqrt(ms + NORM_EPS) * hnw_ref[...]).astype(BF16)
        return _dot(yn, wout_ref[...])

    def residual(r, out):
        xn = (xo_ref[r, :] if moe_in else x_ref[r, :]) + out
        xo_ref[r, :] = xn
        h2 = _rms(xn, fnw_ref[...])
        h2_ref[r, :] = h2.astype(h2_ref.dtype)
        return _dot(h2.astype(BF16), rw_ref[...]) if router else None

    def top2(r, lg):
        lanef = lane.astype(F32)
        l1 = jnp.where(lane < N_EXPERTS, lg, -jnp.inf)
        m1 = jnp.max(l1, axis=-1, keepdims=True)
        i1 = jnp.min(jnp.where(l1 == m1, lanef, float(LANES)), axis=-1, keepdims=True)
        l2 = jnp.where(lanef == i1, -jnp.inf, l1)
        m2 = jnp.max(l2, axis=-1, keepdims=True)
        i2 = jnp.min(jnp.where(l2 == m2, lanef, float(LANES)), axis=-1, keepdims=True)
        e2 = jnp.exp(m2 - m1)
        inv = 1.0 / (1.0 + e2)
        out = jnp.where(lane == 0, i1, jnp.where(lane == 1, i2, jnp.where(lane == 2, inv, e2 * inv)))
        rt_ref[r, :] = jnp.where(lane < 4, out, 0.0)

    outs = [mix_out(r) for r in halves]
    logits = [residual(r, out) for r, out in zip(halves, outs)]
    if router:
        for r, lg in zip(halves, logits):
            top2(r, lg)


def _resident(shape, layer=None):
    if layer is None:
        return pl.BlockSpec(shape, lambda *_: (0,) * len(shape), pipeline_mode=pl.Buffered(1))
    return pl.BlockSpec((None,) + shape, lambda *_: (layer,) + (0,) * len(shape), pipeline_mode=pl.Buffered(1))


def _mixer(x, moe, cosf, sinf, anw, win, sinks, convw, hnw, wout, fnw, gsum, rw, layer, seq, h2_dtype):
    n = x.shape[0]
    ts = min(SEQ_TILE, seq)
    nt = seq // ts
    nb = n // seq
    moe_in = moe is not None
    router = rw is not None
    row = lambda b, t: (b * nt + t, 0)
    tile = pl.BlockSpec((ts, D_MODEL), row)
    lanes_tile = pl.BlockSpec((ts, LANES), row)
    in_specs = [tile]
    args = [x]
    if moe_in:
        ymoe, rt_prev = moe
        off = n // ts
        in_specs += [tile, pl.BlockSpec((ts, D_MODEL), lambda b, t: (off + b * nt + t, 0)), lanes_tile]
        args += [ymoe, ymoe, rt_prev]
    rope_tile = pl.BlockSpec((ts, HEAD_DIM // 2), row)
    in_specs += [rope_tile, rope_tile,
                 _resident((1, D_MODEL), layer), _resident((D_MODEL, IN_WIDTH), layer),
                 pl.BlockSpec(memory_space=pltpu.SMEM),
                 _resident((CONV_K, CONV_WIDTH), layer), _resident((1, D_MODEL), layer),
                 _resident((D_MODEL, D_MODEL), layer), _resident((1, D_MODEL), layer),
                 _resident((MXU_DIM, MXU_DIM))]
    args += [cosf, sinf, anw, win, sinks, convw, hnw, wout, fnw, gsum]
    out_shape = [jax.ShapeDtypeStruct((n, D_MODEL), F32), jax.ShapeDtypeStruct((n, D_MODEL), h2_dtype)]
    out_specs = [tile, tile]
    if router:
        in_specs.append(_resident((D_MODEL, LANES)))
        args.append(rw)
        out_shape.append(jax.ShapeDtypeStruct((n, LANES), F32))
        out_specs.append(lanes_tile)
    return pl.pallas_call(
        functools.partial(_mixer_kernel, ts, moe_in, router),
        grid=(nb, nt),
        in_specs=in_specs,
        out_specs=out_specs,
        out_shape=out_shape,
        scratch_shapes=[pltpu.VMEM((ts, ATTN_WIDTH), BF16),
                        pltpu.VMEM((4, WINDOW + ts, LANES), BF16),
                        pltpu.VMEM((4, WINDOW + ts, LANES), BF16),
                        pltpu.VMEM((SUBLANES + ts, CONV_WIDTH), F32),
                        pltpu.VMEM((ts, ATTN_WIDTH), F32),
                        pltpu.VMEM((ts, 3 * CONV_WIDTH), F32),
                        pltpu.VMEM((ts, D_MODEL), BF16)],
        compiler_params=pltpu.CompilerParams(dimension_semantics=("arbitrary", "arbitrary"),
                                             vmem_limit_bytes=VMEM_LIMIT_BYTES),
        name="mixer",
    )(*args)


def _swiglu(hb, wg_ref, wu_ref, wd_ref, acc_ref, side_work=None):
    for c in range(D_FF // FF_CHUNK):
        if side_work is not None:
            side_work(c)
        cols = slice(c * FF_CHUNK, (c + 1) * FF_CHUNK)
        g = _dot(hb, wg_ref[:, cols])
        u = _dot(hb, wu_ref[:, cols])
        a = (jax.nn.silu(g) * u).astype(BF16)
        part = _dot(a, wd_ref[cols, :])
        if c == 0:
            acc_ref[...] = part
        elif c < D_FF // FF_CHUNK - 1:
            acc_ref[...] += part
    return acc_ref[...] + part


def _ffn_kernel(n_cast, x_ref, h_ref, wg_ref, wu_ref, wd_ref, *refs):
    cast_in, o_ref, cast_out, acc_ref = refs[:n_cast], refs[n_cast], refs[n_cast + 1:-1], refs[-1]
    o_ref[...] = x_ref[...] + _swiglu(h_ref[...], wg_ref, wu_ref, wd_ref, acc_ref)
    for src, dst in zip(cast_in, cast_out):
        dst[...] = src[...].astype(dst.dtype)


def _dense_ffn(x, h2, wg, wu, wd, layer, cast=None):
    n = x.shape[0]
    tm = min(FFN_TILE, n)
    steps = n // tm
    tile = pl.BlockSpec((tm, D_MODEL), lambda i: (i, 0))
    in_specs = [tile, tile, _resident((D_MODEL, D_FF), layer), _resident((D_MODEL, D_FF), layer),
                _resident((D_FF, D_MODEL), layer)]
    out_specs = [tile]
    out_shape = [jax.ShapeDtypeStruct((n, D_MODEL), F32)]
    args = [x, h2, wg, wu, wd]
    if cast is not None:
        weights, cast_layer = cast
        per_expert = steps // N_EXPERTS
        assert per_expert * N_EXPERTS == steps
        for w in weights:
            rows, cols = w.shape[2] // per_expert, w.shape[3]
            in_specs.append(pl.BlockSpec((None, None, rows, cols),
                                         lambda i: (cast_layer, i // per_expert, i % per_expert, 0)))
            out_specs.append(pl.BlockSpec((None, rows, cols), lambda i: (i // per_expert, i % per_expert, 0)))
            out_shape.append(jax.ShapeDtypeStruct(w.shape[1:], BF16))
            args.append(w)
    outs = pl.pallas_call(
        functools.partial(_ffn_kernel, len(args) - 5),
        grid=(steps,),
        in_specs=in_specs,
        out_specs=out_specs,
        out_shape=out_shape,
        scratch_shapes=[pltpu.VMEM((tm, D_MODEL), F32)],
        compiler_params=pltpu.CompilerParams(dimension_semantics=("arbitrary",),
                                             vmem_limit_bytes=VMEM_LIMIT_BYTES),
        name="dense_ffn",
    )(*args)
    return outs[0], outs[1:]


def _moe_kernel(blk_e_ref, src0_ref, srcn_ref, dstp_ref, dstl_ref, h_hbm, wg_ref, wu_ref, wd_ref, y_hbm,
                xbuf, ybuf, acc_ref, gsem, ssem):
    i = pl.program_id(0)
    n_real = blk_e_ref[pl.num_programs(0)]
    slot = i % 2
    other = 1 - slot
    bm = MOE_BLOCK
    per_chunk = bm // MOE_ISSUE_CHUNKS

    def gather_row(idx_ref, r, block_slot):
        pltpu.make_async_copy(h_hbm.at[pl.ds(idx_ref[0, 0, r], 1)], xbuf.at[block_slot, pl.ds(r, 1)],
                              gsem.at[block_slot]).start(priority=0)

    def scatter_row(idx_ref, r, block_slot):
        pltpu.make_async_copy(ybuf.at[block_slot, pl.ds(r, 1)], y_hbm.at[pl.ds(idx_ref[0, 0, r], 1)],
                              ssem.at[block_slot]).start(priority=1)

    def wait_gather(block_slot):
        pltpu.make_async_copy(h_hbm.at[pl.ds(0, bm)], xbuf.at[block_slot], gsem.at[block_slot]).wait()

    def wait_scatter(block_slot):
        pltpu.make_async_copy(ybuf.at[block_slot], y_hbm.at[pl.ds(0, bm)], ssem.at[block_slot]).wait()

    @pl.when(i == 0)
    def _():
        def body(r, c):
            gather_row(src0_ref, r, 0)
            return c
        lax.fori_loop(0, bm, body, 0)
        ybuf[...] = jnp.zeros((2, bm, D_MODEL), F32)
        pltpu.make_async_copy(ybuf.at[0], y_hbm.at[pl.ds(y_hbm.shape[0] - bm, bm)], ssem.at[0]).start()

    @pl.when(i < n_real)
    def _():
        wait_gather(slot)

        def side_work(c):
            if c < MOE_ISSUE_CHUNKS:
                for r in range(c * per_chunk, (c + 1) * per_chunk):
                    gather_row(srcn_ref, r, other)
                    scatter_row(dstp_ref, r, other)

        total = _swiglu(xbuf[slot].astype(BF16), wg_ref, wu_ref, wd_ref, acc_ref, side_work)
        wait_scatter(slot)
        ybuf[slot] = total

    @pl.when(i >= n_real)
    def _():
        ybuf[0] = jnp.zeros((bm, D_MODEL), F32)
        fill = pltpu.make_async_copy(ybuf.at[0], y_hbm.at[pl.ds(pl.multiple_of(i * bm, bm), bm)], ssem.at[0])
        fill.start()
        fill.wait()

    @pl.when(i == n_real - 1)
    def _():
        def body(r, c):
            scatter_row(dstl_ref, r, slot)
            return c
        lax.fori_loop(0, bm, body, 0)
        wait_scatter(slot)
        wait_scatter(other)
        wait_gather(other)


def _moe_experts(h2, blk_e, src, dst, wg, wu, wd):
    nblocks = src.shape[0]
    bm = MOE_BLOCK
    n_rows = (nblocks + 2) * bm
    wspec_in = pl.BlockSpec((None, D_MODEL, D_FF), lambda i, be: (be[i], 0, 0), pipeline_mode=pl.Buffered(1))
    wspec_out = pl.BlockSpec((None, D_FF, D_MODEL), lambda i, be: (be[i], 0, 0), pipeline_mode=pl.Buffered(1))
    smem_rows = lambda index_map: pl.BlockSpec((1, 1, bm), index_map, memory_space=pltpu.SMEM)
    grid_spec = pltpu.PrefetchScalarGridSpec(
        num_scalar_prefetch=1,
        grid=(nblocks,),
        in_specs=[smem_rows(lambda i, be: (0, 0, 0)),
                  smem_rows(lambda i, be: (jnp.minimum(i + 1, nblocks - 1), 0, 0)),
                  smem_rows(lambda i, be: (i, 0, 0)),
                  smem_rows(lambda i, be: (be[nblocks], 0, 0)),
                  pl.BlockSpec(memory_space=pl.ANY),
                  wspec_in, wspec_in, wspec_out],
        out_specs=pl.BlockSpec(memory_space=pl.ANY),
        scratch_shapes=[pltpu.VMEM((2, bm, D_MODEL), F32),
                        pltpu.VMEM((2, bm, D_MODEL), F32),
                        pltpu.VMEM((bm, D_MODEL), F32),
                        pltpu.SemaphoreType.DMA((2,)),
                        pltpu.SemaphoreType.DMA((2,))],
    )
    return pl.pallas_call(
        _moe_kernel,
        grid_spec=grid_spec,
        out_shape=jax.ShapeDtypeStruct((n_rows, D_MODEL), F32),
        compiler_params=pltpu.CompilerParams(dimension_semantics=("arbitrary",),
                                             vmem_limit_bytes=VMEM_LIMIT_BYTES),
        name="moe_experts",
    )(blk_e, src, src, dst, dst, h2, wg, wu, wd)


def _route(rt, n):
    bm = MOE_BLOCK
    n_assign = n * TOP_K
    n_rows = n_assign + N_EXPERTS * bm
    nblocks = n_rows // bm
    flat_e = rt[:, 0:TOP_K].astype(jnp.int32).reshape(-1)
    order = jnp.argsort(flat_e).astype(jnp.int32)
    experts = jnp.arange(N_EXPERTS, dtype=jnp.int32)
    sizes = jnp.sum((flat_e[:, None] == experts[None, :]).astype(jnp.int32), axis=0)
    padded = (sizes + bm - 1) // bm * bm
    group_end = jnp.cumsum(sizes)
    padded_end = jnp.cumsum(padded)
    blk_start = jnp.arange(nblocks, dtype=jnp.int32) * bm
    blk_e = jnp.minimum(jnp.sum((blk_start[:, None] >= padded_end[None, :]).astype(jnp.int32), axis=1),
                        N_EXPERTS - 1)
    rank = blk_start[:, None] + jnp.arange(bm, dtype=jnp.int32)[None, :] - (padded_end - padded)[blk_e][:, None]
    valid = rank < sizes[blk_e][:, None]
    sorted_pos = (group_end - sizes)[blk_e][:, None] + rank
    assign = order[jnp.clip(sorted_pos, 0, n_assign - 1)]
    tok = assign // TOP_K
    src = jnp.where(valid, tok, 0)
    slot = blk_start[:, None] + jnp.arange(bm, dtype=jnp.int32)[None, :]
    dst = jnp.where(valid, (assign % TOP_K) * n + tok, n_assign + slot - group_end[blk_e][:, None])
    spare = n_rows + jnp.arange(bm, dtype=jnp.int32)
    dst = jnp.concatenate([spare[None, :], dst], axis=0)
    n_real = padded_end[-1:] // bm
    return jnp.concatenate([blk_e, n_real]), src.reshape(nblocks, 1, bm), dst.reshape(nblocks + 1, 1, bm)


def _final_kernel(moe_in, *refs):
    if moe_in:
        x_ref, ya_ref, yb_ref, rt_ref, w_ref, o_ref = refs
        x = _moe_combine(x_ref[...], ya_ref, yb_ref, rt_ref)
    else:
        x_ref, w_ref, o_ref = refs
        x = x_ref[...]
    o_ref[...] = _rms(x, w_ref[...])


def _final(x, moe, w):
    n = x.shape[0]
    tm = min(1024, n)
    off = n // tm
    tile = pl.BlockSpec((tm, D_MODEL), lambda i: (i, 0))
    in_specs = [tile]
    args = [x]
    if moe is not None:
        in_specs += [tile, pl.BlockSpec((tm, D_MODEL), lambda i: (off + i, 0)),
                     pl.BlockSpec((tm, LANES), lambda i: (i, 0))]
        args += [moe[0], moe[0], moe[1]]
    return pl.pallas_call(
        functools.partial(_final_kernel, moe is not None),
        grid=(n // tm,),
        in_specs=in_specs + [_resident((1, D_MODEL))],
        out_specs=tile,
        out_shape=jax.ShapeDtypeStruct((n, D_MODEL), F32),
        compiler_params=pltpu.CompilerParams(dimension_semantics=("arbitrary",)),
        name="final_norm",
    )(*args, w)


def kernel(x, positions, attn_norm_w, w_in, attn_sinks, conv_w, head_norm_w, w_out, ffn_norm_w, dense_w_gate,
           dense_w_up, dense_w_down, router_w, moe_w_gate, moe_w_up, moe_w_down, final_norm_w):
    bsz, seq, _ = x.shape
    n = bsz * seq
    depth = w_in.shape[0]

    inv_freq = ROPE_THETA ** (-jnp.arange(0, HEAD_DIM, 2, dtype=F32) / HEAD_DIM)
    ang = positions.astype(F32).reshape(n, 1) * inv_freq
    cosf, sinf = jnp.cos(ang), jnp.sin(ang)

    grp = jnp.arange(MXU_DIM) // HEAD_DIM
    gsum = jnp.where(grp[:, None] == grp[None, :], 1.0 / HEAD_DIM, 0.0).astype(BF16)

    win_b, wout_b = w_in.astype(BF16), w_out.astype(BF16)
    dense_b = [w.astype(BF16) for w in (dense_w_gate, dense_w_up, dense_w_down)]
    moe_f = (moe_w_gate, moe_w_up, moe_w_down)
    rw_b = jnp.pad(router_w, ((0, 0), (0, 0), (0, LANES - N_EXPERTS))).astype(BF16)
    anw, hnw, fnw = (w[:, None, :] for w in (attn_norm_w, head_norm_w, ffn_norm_w))

    cast_in_dense = (n // min(FFN_TILE, n)) % N_EXPERTS == 0

    xf = x.reshape(n, D_MODEL)
    moe = None
    moe_b = ()
    for l in range(depth):
        is_moe = l % 2 == 1
        i = l // 2
        outs = _mixer(xf, moe, cosf, sinf, anw, win_b, attn_sinks[l], conv_w, hnw, wout_b, fnw, gsum,
                      rw_b[i] if is_moe else None, l, seq, F32 if is_moe else BF16)
        if is_moe:
            xf, h2, rt = outs
            blk_e, src, dst = _route(rt, n)
            if not moe_b:
                moe_b = [w[i].astype(BF16) for w in moe_f]
            moe = (_moe_experts(h2, blk_e, src, dst, *moe_b), rt)
            moe_b = ()
        else:
            xf, h2 = outs
            cast = (moe_f, i) if cast_in_dense and l + 1 < depth else None
            xf, moe_b = _dense_ffn(xf, h2, *dense_b, i, cast)
            moe = None
    out = _final(xf, moe, final_norm_w[None])
    return out.reshape(bsz, seq, D_MODEL)
```

```python
import functools

import jax
import jax.numpy as jnp
from jax import lax
from jax.experimental import pallas as pl
from jax.experimental.pallas import tpu as pltpu

D_MODEL = 1024
HEAD_DIM = 64
N_Q_HEADS = 8
N_KV_HEADS = 2
ATTN_WIDTH = N_Q_HEADS * HEAD_DIM
KV_WIDTH = N_KV_HEADS * HEAD_DIM
CONV_WIDTH = D_MODEL - ATTN_WIDTH
Q_END = ATTN_WIDTH
K_END = Q_END + KV_WIDTH
V_END = K_END + KV_WIDTH
CB_END = V_END + CONV_WIDTH
CC_END = CB_END + CONV_WIDTH
IN_WIDTH = CC_END + CONV_WIDTH
WINDOW = 128
ROPE_THETA = 10000.0
CONV_K = 3
D_FF = 3584
N_EXPERTS = 8
TOP_K = 2
NORM_EPS = 1e-5
NEG_INF = -1e30

LANES = 128
SUBLANES = 8
MXU_DIM = 256
ROPE_PACK = LANES // (HEAD_DIM // 2)
VMEM_LIMIT_BYTES = 58 * 1024 * 1024

SEQ_TILE = 512
FFN_TILE = 512
FF_CHUNK = 1792
MOE_BLOCK = 512
MOE_ISSUE_CHUNKS = min(4, D_FF // FF_CHUNK)

F32 = jnp.float32
BF16 = jnp.bfloat16


def _rms(x, w):
    var = jnp.mean(x * x, axis=-1, keepdims=True)
    return x * lax.rsqrt(var + NORM_EPS) * w


def _dot(a, b):
    return jnp.dot(a, b, preferred_element_type=F32)


def _dot_nt(a, b):
    return lax.dot_general(a, b, (((1,), (1,)), ((), ())), preferred_element_type=F32)


def _moe_combine(x, ya_ref, yb_ref, rt_ref):
    rt = rt_ref[...]
    return x + rt[:, TOP_K:TOP_K + 1] * ya_ref[...] + rt[:, TOP_K + 1:TOP_K + 2] * yb_ref[...]


def _mixer_kernel(ts, moe_in, router, *refs):
    refs = list(refs)
    x_ref = refs.pop(0)
    ya_ref = refs.pop(0) if moe_in else None
    yb_ref = refs.pop(0) if moe_in else None
    rtp_ref = refs.pop(0) if moe_in else None
    (cos_ref, sin_ref, anw_ref, win_ref, sinks_ref, convw_ref, hnw_ref, wout_ref, fnw_ref,
     gsum_ref) = refs[:10]
    refs = refs[10:]
    rw_ref = refs.pop(0) if router else None
    xo_ref = refs.pop(0)
    h2_ref = refs.pop(0)
    rt_ref = refs.pop(0) if router else None
    q_s, k_s, v_s, u_s, y_s, c_s, h_s, rc_s, rs_s = refs

    t = pl.program_id(1)
    nblk = ts // WINDOW

    @pl.when(t == 0)
    def _():
        k_s[:, 0:WINDOW, :] = jnp.zeros((4, WINDOW, LANES), BF16)
        v_s[:, 0:WINDOW, :] = jnp.zeros((4, WINDOW, LANES), BF16)
        u_s[0:SUBLANES, :] = jnp.zeros((SUBLANES, CONV_WIDTH), F32)

    @pl.when(t > 0)
    def _():
        k_s[:, 0:WINDOW, :] = k_s[:, ts:ts + WINDOW, :]
        v_s[:, 0:WINDOW, :] = v_s[:, ts:ts + WINDOW, :]
        u_s[0:SUBLANES, :] = u_s[ts:ts + SUBLANES, :]

    hr = ts // 2
    halves = [slice(0, hr), slice(hr, ts)]
    lane = lax.broadcasted_iota(jnp.int32, (hr, LANES), 1)
    first_half = (lane & (HEAD_DIM // 2)) == 0
    low_head = lane < HEAD_DIM
    scale = HEAD_DIM ** -0.5

    def qkv_proj(r):
        xh = x_ref[r, :]
        if moe_in:
            xh = _moe_combine(xh, ya_ref.at[r, :], yb_ref.at[r, :], rtp_ref.at[r, :])
            xo_ref[r, :] = xh
        hh = _rms(xh, anw_ref[...]).astype(BF16)
        h_s[r, :] = hh
        return _dot(hh, win_ref[:, 0:V_END])

    def rope_and_stage(r, proj):
        def per_token(tab_ref, dst):
            tab = tab_ref[r.start // ROPE_PACK:r.stop // ROPE_PACK, :]
            for k in range(ROPE_PACK):
                piece = tab[:, k * (HEAD_DIM // 2):(k + 1) * (HEAD_DIM // 2)]
                dst[pl.ds(r.start + k, hr // ROPE_PACK, stride=ROPE_PACK), :] = jnp.concatenate(
                    [piece] * ROPE_PACK, axis=1)
            return dst[r, :]

        cosf = per_token(cos_ref, rc_s)
        sin4 = per_token(sin_ref, rs_s)
        sinf = jnp.where(first_half, -sin4, sin4)

        def rope(tq):
            rot = jnp.where(first_half, pltpu.roll(tq, LANES - HEAD_DIM // 2, 1), pltpu.roll(tq, HEAD_DIM // 2, 1))
            return tq * cosf + rot * sinf

        for p in range(ATTN_WIDTH // LANES):
            q_s[r, p * LANES:(p + 1) * LANES] = (rope(proj[:, p * LANES:(p + 1) * LANES]) * scale).astype(BF16)

        def lane_variants(a, dst):
            ar = pltpu.roll(a, HEAD_DIM, 1)
            zero = jnp.zeros_like(a)
            rows = slice(WINDOW + r.start, WINDOW + r.stop)
            dst[0, rows, :] = jnp.where(low_head, a, zero).astype(BF16)
            dst[1, rows, :] = jnp.where(low_head, zero, ar).astype(BF16)
            dst[2, rows, :] = jnp.where(low_head, ar, zero).astype(BF16)
            dst[3, rows, :] = jnp.where(low_head, zero, a).astype(BF16)

        lane_variants(rope(proj[:, Q_END:K_END]), k_s)
        lane_variants(proj[:, K_END:V_END], v_s)

    projs = [qkv_proj(r) for r in halves]
    for r, proj in zip(halves, projs):
        rope_and_stage(r, proj)

    qi = lax.broadcasted_iota(jnp.int32, (WINDOW, 2 * WINDOW), 0)
    kj = lax.broadcasted_iota(jnp.int32, (WINDOW, 2 * WINDOW), 1)
    band = (kj > qi) & (kj <= qi + WINDOW)
    band_first = band & (kj >= jnp.where(t > 0, 0, WINDOW))

    def scores(b, g):
        rows = slice(b * WINDOW, (b + 1) * WINDOW)
        ctx = slice(b * WINDOW, (b + 2) * WINDOW)
        qg = jnp.concatenate([q_s[rows, (2 * g) * LANES:(2 * g + 1) * LANES],
                              q_s[rows, (2 * g + 1) * LANES:(2 * g + 2) * LANES]], axis=0)
        return [_dot_nt(qg, k_s[2 * g + j, ctx, :]) for j in range(2)]

    steps = [(b, g) for b in range(nblk) for g in range(N_KV_HEADS)]
    s_next = scores(*steps[0])
    for idx, (b, g) in enumerate(steps):
        mask = band_first if b == 0 else band
        rows = slice(b * WINDOW, (b + 1) * WINDOW)
        ctx = slice(b * WINDOW, (b + 2) * WINDOW)
        s_cur = s_next
        if idx < 3 * CONV_WIDTH // MXU_DIM:
            ccols = slice(idx * MXU_DIM, (idx + 1) * MXU_DIM)
            c_s[:, ccols] = _dot(h_s[...], win_ref[:, V_END + idx * MXU_DIM:V_END + (idx + 1) * MXU_DIM])
        if idx + 1 < len(steps):
            s_next = scores(*steps[idx + 1])
        probs = []
        for j in range(2):
            head_probs = []
            for r in range(2):
                sink = sinks_ref[4 * g + 2 * r + j]
                sr = jnp.where(mask, s_cur[j][r * WINDOW:(r + 1) * WINDOW], NEG_INF)
                m = jnp.maximum(jnp.max(sr, axis=-1, keepdims=True), sink)
                e = jnp.exp(sr - m)
                denom = jnp.sum(e, axis=-1, keepdims=True) + jnp.exp(sink - m)
                head_probs.append((e * (1.0 / denom)).astype(BF16))
            probs.append(jnp.concatenate(head_probs, axis=0))
        o = _dot(probs[0], v_s[2 * g, ctx, :]) + _dot(probs[1], v_s[2 * g + 1, ctx, :])
        y_s[rows, (2 * g) * LANES:(2 * g + 1) * LANES] = o[0:WINDOW]
        y_s[rows, (2 * g + 1) * LANES:(2 * g + 2) * LANES] = o[WINDOW:2 * WINDOW]

    for cchunk in range(nblk * N_KV_HEADS, 3 * CONV_WIDTH // MXU_DIM):
        c_s[:, cchunk * MXU_DIM:(cchunk + 1) * MXU_DIM] = _dot(
            h_s[...], win_ref[:, V_END + cchunk * MXU_DIM:V_END + (cchunk + 1) * MXU_DIM])

    u_s[SUBLANES:SUBLANES + ts, :] = c_s[:, CONV_WIDTH:2 * CONV_WIDTH] * c_s[:, 2 * CONV_WIDTH:3 * CONV_WIDTH]
    cw = convw_ref[...]

    def mix_out(r):
        conv = cw[0:1, :] * u_s[SUBLANES - 2 + r.start:SUBLANES - 2 + r.stop, :]
        conv = conv + cw[1:2, :] * u_s[SUBLANES - 1 + r.start:SUBLANES - 1 + r.stop, :]
        conv = conv + cw[2:3, :] * u_s[SUBLANES + r.start:SUBLANES + r.stop, :]
        y = jnp.concatenate([y_s[r, :], c_s[r, 0:CONV_WIDTH] * conv], axis=-1)
        sq = (y * y).astype(BF16)
        ms = jnp.concatenate([_dot(sq[:, c * MXU_DIM:(c + 1) * MXU_DIM], gsum_ref[...])
                              for c in range(D_MODEL // MXU_DIM)], axis=-1)
        yn = (y * lax.rsqrt(ms + NORM_EPS) * hnw_ref[...]).astype(BF16)
        return _dot(yn, wout_ref[...])

    def residual(r, out):
        xn = (xo_ref[r, :] if moe_in else x_ref[r, :]) + out
        xo_ref[r, :] = xn
        h2 = _rms(xn, fnw_ref[...])
        h2_ref[r, :] = h2.astype(h2_ref.dtype)
        return _dot(h2.astype(BF16), rw_ref[...]) if router else None

    def top2(r, lg):
        lanef = lane.astype(F32)
        l1 = jnp.where(lane < N_EXPERTS, lg, -jnp.inf)
        m1 = jnp.max(l1, axis=-1, keepdims=True)
        i1 = jnp.min(jnp.where(l1 == m1, lanef, float(LANES)), axis=-1, keepdims=True)
        l2 = jnp.where(lanef == i1, -jnp.inf, l1)
        m2 = jnp.max(l2, axis=-1, keepdims=True)
        i2 = jnp.min(jnp.where(l2 == m2, lanef, float(LANES)), axis=-1, keepdims=True)
        e2 = jnp.exp(m2 - m1)
        inv = 1.0 / (1.0 + e2)
        out = jnp.where(lane == 0, i1, jnp.where(lane == 1, i2, jnp.where(lane == 2, inv, e2 * inv)))
        rt_ref[r, :] = jnp.where(lane < 4, out, 0.0)

    outs = [mix_out(r) for r in halves]
    logits = [residual(r, out) for r, out in zip(halves, outs)]
    if router:
        for r, lg in zip(halves, logits):
            top2(r, lg)


def _resident(shape, layer=None):
    if layer is None:
        return pl.BlockSpec(shape, lambda *_: (0,) * len(shape), pipeline_mode=pl.Buffered(1))
    return pl.BlockSpec((None,) + shape, lambda *_: (layer,) + (0,) * len(shape), pipeline_mode=pl.Buffered(1))


def _mixer(x, moe, cosf, sinf, anw, win, sinks, convw, hnw, wout, fnw, gsum, rw, layer, seq, h2_dtype):
    n = x.shape[0]
    ts = min(SEQ_TILE, seq)
    nt = seq // ts
    nb = n // seq
    moe_in = moe is not None
    router = rw is not None
    row = lambda b, t: (b * nt + t, 0)
    tile = pl.BlockSpec((ts, D_MODEL), row)
    lanes_tile = pl.BlockSpec((ts, LANES), row)
    in_specs = [tile]
    args = [x]
    if moe_in:
        ymoe, rt_prev = moe
        off = n // ts
        in_specs += [tile, pl.BlockSpec((ts, D_MODEL), lambda b, t: (off + b * nt + t, 0)), lanes_tile]
        args += [ymoe, ymoe, rt_prev]
    rope_tile = pl.BlockSpec((ts // ROPE_PACK, LANES), row)
    in_specs += [rope_tile, rope_tile,
                 _resident((1, D_MODEL), layer), _resident((D_MODEL, IN_WIDTH), layer),
                 pl.BlockSpec(memory_space=pltpu.SMEM),
                 _resident((CONV_K, CONV_WIDTH), layer), _resident((1, D_MODEL), layer),
                 _resident((D_MODEL, D_MODEL), layer), _resident((1, D_MODEL), layer),
                 _resident((MXU_DIM, MXU_DIM))]
    args += [cosf, sinf, anw, win, sinks, convw, hnw, wout, fnw, gsum]
    out_shape = [jax.ShapeDtypeStruct((n, D_MODEL), F32), jax.ShapeDtypeStruct((n, D_MODEL), h2_dtype)]
    out_specs = [tile, tile]
    if router:
        in_specs.append(_resident((D_MODEL, LANES)))
        args.append(rw)
        out_shape.append(jax.ShapeDtypeStruct((n, LANES), F32))
        out_specs.append(lanes_tile)
    return pl.pallas_call(
        functools.partial(_mixer_kernel, ts, moe_in, router),
        grid=(nb, nt),
        in_specs=in_specs,
        out_specs=out_specs,
        out_shape=out_shape,
        scratch_shapes=[pltpu.VMEM((ts, ATTN_WIDTH), BF16),
                        pltpu.VMEM((4, WINDOW + ts, LANES), BF16),
                        pltpu.VMEM((4, WINDOW + ts, LANES), BF16),
                        pltpu.VMEM((SUBLANES + ts, CONV_WIDTH), F32),
                        pltpu.VMEM((ts, ATTN_WIDTH), F32),
                        pltpu.VMEM((ts, 3 * CONV_WIDTH), F32),
                        pltpu.VMEM((ts, D_MODEL), BF16),
                        pltpu.VMEM((ts, LANES), F32),
                        pltpu.VMEM((ts, LANES), F32)],
        compiler_params=pltpu.CompilerParams(dimension_semantics=("arbitrary", "arbitrary"),
                                             vmem_limit_bytes=VMEM_LIMIT_BYTES),
        name="mixer",
    )(*args)


def _swiglu(hb, wg_ref, wu_ref, wd_ref, acc_ref, side_work=None):
    for c in range(D_FF // FF_CHUNK):
        if side_work is not None:
            side_work(c)
        cols = slice(c * FF_CHUNK, (c + 1) * FF_CHUNK)
        g = _dot(hb, wg_ref[:, cols])
        u = _dot(hb, wu_ref[:, cols])
        a = (jax.nn.silu(g) * u).astype(BF16)
        part = _dot(a, wd_ref[cols, :])
        if c == 0:
            acc_ref[...] = part
        elif c < D_FF // FF_CHUNK - 1:
            acc_ref[...] += part
    return acc_ref[...] + part


def _ffn_kernel(n_cast, x_ref, h_ref, wg_ref, wu_ref, wd_ref, *refs):
    cast_in, o_ref, cast_out, acc_ref = refs[:n_cast], refs[n_cast], refs[n_cast + 1:-1], refs[-1]
    o_ref[...] = x_ref[...] + _swiglu(h_ref[...], wg_ref, wu_ref, wd_ref, acc_ref)
    for src, dst in zip(cast_in, cast_out):
        dst[...] = src[...].astype(dst.dtype)


def _dense_ffn(x, h2, wg, wu, wd, layer, cast=None):
    n = x.shape[0]
    tm = min(FFN_TILE, n)
    steps = n // tm
    tile = pl.BlockSpec((tm, D_MODEL), lambda i: (i, 0))
    in_specs = [tile, tile, _resident((D_MODEL, D_FF), layer), _resident((D_MODEL, D_FF), layer),
                _resident((D_FF, D_MODEL), layer)]
    out_specs = [tile]
    out_shape = [jax.ShapeDtypeStruct((n, D_MODEL), F32)]
    args = [x, h2, wg, wu, wd]
    if cast is not None:
        weights, cast_layer = cast
        per_expert = steps // N_EXPERTS
        assert per_expert * N_EXPERTS == steps
        for w in weights:
            rows, cols = w.shape[2] // per_expert, w.shape[3]
            in_specs.append(pl.BlockSpec((None, None, rows, cols),
                                         lambda i: (cast_layer, i // per_expert, i % per_expert, 0)))
            out_specs.append(pl.BlockSpec((None, rows, cols), lambda i: (i // per_expert, i % per_expert, 0)))
            out_shape.append(jax.ShapeDtypeStruct(w.shape[1:], BF16))
            args.append(w)
    outs = pl.pallas_call(
        functools.partial(_ffn_kernel, len(args) - 5),
        grid=(steps,),
        in_specs=in_specs,
        out_specs=out_specs,
        out_shape=out_shape,
        scratch_shapes=[pltpu.VMEM((tm, D_MODEL), F32)],
        compiler_params=pltpu.CompilerParams(dimension_semantics=("arbitrary",),
                                             vmem_limit_bytes=VMEM_LIMIT_BYTES),
        name="dense_ffn",
    )(*args)
    return outs[0], outs[1:]


def _moe_kernel(blk_e_ref, src0_ref, srcn_ref, dstp_ref, dstl_ref, h_hbm, wg_ref, wu_ref, wd_ref, y_hbm,
                xbuf, ybuf, acc_ref, gsem, ssem):
    i = pl.program_id(0)
    n_real = blk_e_ref[pl.num_programs(0)]
    slot = i % 2
    other = 1 - slot
    bm = MOE_BLOCK
    per_chunk = bm // MOE_ISSUE_CHUNKS

    def gather_row(idx_ref, r, block_slot):
        pltpu.make_async_copy(h_hbm.at[pl.ds(idx_ref[0, 0, r], 1)], xbuf.at[block_slot, pl.ds(r, 1)],
                              gsem.at[block_slot]).start(priority=0)

    def scatter_row(idx_ref, r, block_slot):
        pltpu.make_async_copy(ybuf.at[block_slot, pl.ds(r, 1)], y_hbm.at[pl.ds(idx_ref[0, 0, r], 1)],
                              ssem.at[block_slot]).start(priority=1)

    def wait_gather(block_slot):
        pltpu.make_async_copy(h_hbm.at[pl.ds(0, bm)], xbuf.at[block_slot], gsem.at[block_slot]).wait()

    def wait_scatter(block_slot):
        pltpu.make_async_copy(ybuf.at[block_slot], y_hbm.at[pl.ds(0, bm)], ssem.at[block_slot]).wait()

    @pl.when(i == 0)
    def _():
        def body(r, c):
            gather_row(src0_ref, r, 0)
            return c
        lax.fori_loop(0, bm, body, 0)
        ybuf[...] = jnp.zeros((2, bm, D_MODEL), F32)
        pltpu.make_async_copy(ybuf.at[0], y_hbm.at[pl.ds(y_hbm.shape[0] - bm, bm)], ssem.at[0]).start()

    @pl.when(i < n_real)
    def _():
        wait_gather(slot)

        def side_work(c):
            if c < MOE_ISSUE_CHUNKS:
                for r in range(c * per_chunk, (c + 1) * per_chunk):
                    gather_row(srcn_ref, r, other)
                    scatter_row(dstp_ref, r, other)

        total = _swiglu(xbuf[slot].astype(BF16), wg_ref, wu_ref, wd_ref, acc_ref, side_work)
        wait_scatter(slot)
        ybuf[slot] = total

    @pl.when(i >= n_real)
    def _():
        ybuf[0] = jnp.zeros((bm, D_MODEL), F32)
        fill = pltpu.make_async_copy(ybuf.at[0], y_hbm.at[pl.ds(pl.multiple_of(i * bm, bm), bm)], ssem.at[0])
        fill.start()
        fill.wait()

    @pl.when(i == n_real - 1)
    def _():
        def body(r, c):
            scatter_row(dstl_ref, r, slot)
            return c
        lax.fori_loop(0, bm, body, 0)
        wait_scatter(slot)
        wait_scatter(other)
        wait_gather(other)


def _moe_experts(h2, blk_e, src, dst, wg, wu, wd):
    nblocks = src.shape[0]
    bm = MOE_BLOCK
    n_rows = (nblocks + 2) * bm
    wspec_in = pl.BlockSpec((None, D_MODEL, D_FF), lambda i, be: (be[i], 0, 0), pipeline_mode=pl.Buffered(1))
    wspec_out = pl.BlockSpec((None, D_FF, D_MODEL), lambda i, be: (be[i], 0, 0), pipeline_mode=pl.Buffered(1))
    smem_rows = lambda index_map: pl.BlockSpec((1, 1, bm), index_map, memory_space=pltpu.SMEM)
    grid_spec = pltpu.PrefetchScalarGridSpec(
        num_scalar_prefetch=1,
        grid=(nblocks,),
        in_specs=[smem_rows(lambda i, be: (0, 0, 0)),
                  smem_rows(lambda i, be: (jnp.minimum(i + 1, nblocks - 1), 0, 0)),
                  smem_rows(lambda i, be: (i, 0, 0)),
                  smem_rows(lambda i, be: (be[nblocks], 0, 0)),
                  pl.BlockSpec(memory_space=pl.ANY),
                  wspec_in, wspec_in, wspec_out],
        out_specs=pl.BlockSpec(memory_space=pl.ANY),
        scratch_shapes=[pltpu.VMEM((2, bm, D_MODEL), F32),
                        pltpu.VMEM((2, bm, D_MODEL), F32),
                        pltpu.VMEM((bm, D_MODEL), F32),
                        pltpu.SemaphoreType.DMA((2,)),
                        pltpu.SemaphoreType.DMA((2,))],
    )
    return pl.pallas_call(
        _moe_kernel,
        grid_spec=grid_spec,
        out_shape=jax.ShapeDtypeStruct((n_rows, D_MODEL), F32),
        compiler_params=pltpu.CompilerParams(dimension_semantics=("arbitrary",),
                                             vmem_limit_bytes=VMEM_LIMIT_BYTES),
        name="moe_experts",
    )(blk_e, src, src, dst, dst, h2, wg, wu, wd)


def _route(rt, n):
    bm = MOE_BLOCK
    n_assign = n * TOP_K
    n_rows = n_assign + N_EXPERTS * bm
    nblocks = n_rows // bm
    flat_e = rt[:, 0:TOP_K].astype(jnp.int32).reshape(-1)
    order = jnp.argsort(flat_e).astype(jnp.int32)
    experts = jnp.arange(N_EXPERTS, dtype=jnp.int32)
    sizes = jnp.sum((flat_e[:, None] == experts[None, :]).astype(jnp.int32), axis=0)
    padded = (sizes + bm - 1) // bm * bm
    group_end = jnp.cumsum(sizes)
    padded_end = jnp.cumsum(padded)
    blk_start = jnp.arange(nblocks, dtype=jnp.int32) * bm
    blk_e = jnp.minimum(jnp.sum((blk_start[:, None] >= padded_end[None, :]).astype(jnp.int32), axis=1),
                        N_EXPERTS - 1)
    rank = blk_start[:, None] + jnp.arange(bm, dtype=jnp.int32)[None, :] - (padded_end - padded)[blk_e][:, None]
    valid = rank < sizes[blk_e][:, None]
    sorted_pos = (group_end - sizes)[blk_e][:, None] + rank
    assign = order[jnp.clip(sorted_pos, 0, n_assign - 1)]
    tok = assign // TOP_K
    src = jnp.where(valid, tok, 0)
    slot = blk_start[:, None] + jnp.arange(bm, dtype=jnp.int32)[None, :]
    dst = jnp.where(valid, (assign % TOP_K) * n + tok, n_assign + slot - group_end[blk_e][:, None])
    spare = n_rows + jnp.arange(bm, dtype=jnp.int32)
    dst = jnp.concatenate([spare[None, :], dst], axis=0)
    n_real = padded_end[-1:] // bm
    return jnp.concatenate([blk_e, n_real]), src.reshape(nblocks, 1, bm), dst.reshape(nblocks + 1, 1, bm)


def _final_kernel(moe_in, *refs):
    if moe_in:
        x_ref, ya_ref, yb_ref, rt_ref, w_ref, o_ref = refs
        x = _moe_combine(x_ref[...], ya_ref, yb_ref, rt_ref)
    else:
        x_ref, w_ref, o_ref = refs
        x = x_ref[...]
    o_ref[...] = _rms(x, w_ref[...])


def _final(x, moe, w):
    n = x.shape[0]
    tm = min(1024, n)
    off = n // tm
    tile = pl.BlockSpec((tm, D_MODEL), lambda i: (i, 0))
    in_specs = [tile]
    args = [x]
    if moe is not None:
        in_specs += [tile, pl.BlockSpec((tm, D_MODEL), lambda i: (off + i, 0)),
                     pl.BlockSpec((tm, LANES), lambda i: (i, 0))]
        args += [moe[0], moe[0], moe[1]]
    return pl.pallas_call(
        functools.partial(_final_kernel, moe is not None),
        grid=(n // tm,),
        in_specs=in_specs + [_resident((1, D_MODEL))],
        out_specs=tile,
        out_shape=jax.ShapeDtypeStruct((n, D_MODEL), F32),
        compiler_params=pltpu.CompilerParams(dimension_semantics=("arbitrary",)),
        name="final_norm",
    )(*args, w)


def kernel(x, positions, attn_norm_w, w_in, attn_sinks, conv_w, head_norm_w, w_out, ffn_norm_w, dense_w_gate,
           dense_w_up, dense_w_down, router_w, moe_w_gate, moe_w_up, moe_w_down, final_norm_w):
    bsz, seq, _ = x.shape
    n = bsz * seq
    depth = w_in.shape[0]

    inv_freq = ROPE_THETA ** (-jnp.arange(0, HEAD_DIM, 2, dtype=F32) / HEAD_DIM)
    ang = (jnp.repeat(positions.astype(F32).reshape(n // ROPE_PACK, ROPE_PACK), HEAD_DIM // 2, axis=1)
           * jnp.tile(inv_freq, ROPE_PACK))
    cosf, sinf = jnp.cos(ang), jnp.sin(ang)

    grp = jnp.arange(MXU_DIM) // HEAD_DIM
    gsum = jnp.where(grp[:, None] == grp[None, :], 1.0 / HEAD_DIM, 0.0).astype(BF16)

    win_b, wout_b = w_in.astype(BF16), w_out.astype(BF16)
    dense_b = [w.astype(BF16) for w in (dense_w_gate, dense_w_up, dense_w_down)]
    moe_f = (moe_w_gate, moe_w_up, moe_w_down)
    rw_b = jnp.pad(router_w, ((0, 0), (0, 0), (0, LANES - N_EXPERTS))).astype(BF16)
    anw, hnw, fnw = (w[:, None, :] for w in (attn_norm_w, head_norm_w, ffn_norm_w))

    cast_in_dense = (n // min(FFN_TILE, n)) % N_EXPERTS == 0

    xf = x.reshape(n, D_MODEL)
    moe = None
    moe_b = ()
    for l in range(depth):
        is_moe = l % 2 == 1
        i = l // 2
        outs = _mixer(xf, moe, cosf, sinf, anw, win_b, attn_sinks[l], conv_w, hnw, wout_b, fnw, gsum,
                      rw_b[i] if is_moe else None, l, seq, F32 if is_moe else BF16)
        if is_moe:
            xf, h2, rt = outs
            blk_e, src, dst = _route(rt, n)
            if not moe_b:
                moe_b = [w[i].astype(BF16) for w in moe_f]
            moe = (_moe_experts(h2, blk_e, src, dst, *moe_b), rt)
            moe_b = ()
        else:
            xf, h2 = outs
            cast = (moe_f, i) if cast_in_dense and l + 1 < depth else None
            xf, moe_b = _dense_ffn(xf, h2, *dense_b, i, cast)
            moe = None
    out = _final(xf, moe, final_norm_w[None])
    return out.reshape(bsz, seq, D_MODEL)
```

```python
import functools

import jax
import jax.numpy as jnp
from jax import lax
from jax.experimental import pallas as pl
from jax.experimental.pallas import tpu as pltpu

D_MODEL = 1024
HEAD_DIM = 64
N_Q_HEADS = 8
N_KV_HEADS = 2
ATTN_WIDTH = N_Q_HEADS * HEAD_DIM
KV_WIDTH = N_KV_HEADS * HEAD_DIM
CONV_WIDTH = D_MODEL - ATTN_WIDTH
Q_END = ATTN_WIDTH
K_END = Q_END + KV_WIDTH
V_END = K_END + KV_WIDTH
CB_END = V_END + CONV_WIDTH
CC_END = CB_END + CONV_WIDTH
IN_WIDTH = CC_END + CONV_WIDTH
WINDOW = 128
ROPE_THETA = 10000.0
CONV_K = 3
D_FF = 3584
N_EXPERTS = 8
TOP_K = 2
NORM_EPS = 1e-5
NEG_INF = -1e30

LANES = 128
SUBLANES = 8
MXU_DIM = 256
ROPE_PACK = LANES // (HEAD_DIM // 2)
VMEM_LIMIT_BYTES = 58 * 1024 * 1024

SEQ_TILE = 512
FFN_TILE = 512
FF_CHUNK = 1792
MOE_BLOCK = 512
MOE_ISSUE_CHUNKS = min(4, D_FF // FF_CHUNK)

F32 = jnp.float32
BF16 = jnp.bfloat16


def _rms(x, w):
    var = jnp.mean(x * x, axis=-1, keepdims=True)
    return x * lax.rsqrt(var + NORM_EPS) * w


def _dot(a, b):
    return jnp.dot(a, b, preferred_element_type=F32)


def _dot_nt(a, b):
    return lax.dot_general(a, b, (((1,), (1,)), ((), ())), preferred_element_type=F32)


def _moe_combine(x, ya_ref, yb_ref, rt_ref):
    rt = rt_ref[...]
    return x + rt[:, TOP_K:TOP_K + 1] * ya_ref[...] + rt[:, TOP_K + 1:TOP_K + 2] * yb_ref[...]


def _mixer_kernel(ts, moe_in, router, *refs):
    refs = list(refs)
    x_ref = refs.pop(0)
    ya_ref = refs.pop(0) if moe_in else None
    yb_ref = refs.pop(0) if moe_in else None
    rtp_ref = refs.pop(0) if moe_in else None
    (cos_ref, sin_ref, anw_ref, win_ref, sinks_ref, convw_ref, hnw_ref, wout_ref, fnw_ref,
     gsum_ref) = refs[:10]
    refs = refs[10:]
    rw_ref = refs.pop(0) if router else None
    xo_ref = refs.pop(0)
    h2_ref = refs.pop(0)
    rt_ref = refs.pop(0) if router else None
    q_s, k_s, v_s, u_s, y_s, c_s, h_s, rc_s, rs_s = refs

    t = pl.program_id(1)
    nblk = ts // WINDOW

    @pl.when(t == 0)
    def _():
        k_s[:, 0:WINDOW, :] = jnp.zeros((4, WINDOW, LANES), BF16)
        v_s[:, 0:WINDOW, :] = jnp.zeros((4, WINDOW, LANES), BF16)
        u_s[0:SUBLANES, :] = jnp.zeros((SUBLANES, CONV_WIDTH), F32)

    @pl.when(t > 0)
    def _():
        k_s[:, 0:WINDOW, :] = k_s[:, ts:ts + WINDOW, :]
        v_s[:, 0:WINDOW, :] = v_s[:, ts:ts + WINDOW, :]
        u_s[0:SUBLANES, :] = u_s[ts:ts + SUBLANES, :]

    hr = ts // 2
    halves = [slice(0, hr), slice(hr, ts)]
    lane = lax.broadcasted_iota(jnp.int32, (hr, LANES), 1)
    first_half = (lane & (HEAD_DIM // 2)) == 0
    low_head = lane < HEAD_DIM
    scale = HEAD_DIM ** -0.5

    def qkv_proj(r):
        xh = x_ref[r, :]
        if moe_in:
            xh = _moe_combine(xh, ya_ref.at[r, :], yb_ref.at[r, :], rtp_ref.at[r, :])
            xo_ref[r, :] = xh
        hh = _rms(xh, anw_ref[...]).astype(BF16)
        h_s[r, :] = hh
        return _dot(hh, win_ref[:, 0:V_END])

    def rope_and_stage(r, proj):
        def per_token(tab_ref, dst):
            tab = tab_ref[r.start // ROPE_PACK:r.stop // ROPE_PACK, :]
            for k in range(ROPE_PACK):
                piece = tab[:, k * (HEAD_DIM // 2):(k + 1) * (HEAD_DIM // 2)]
                dst[pl.ds(r.start + k, hr // ROPE_PACK, stride=ROPE_PACK), :] = jnp.concatenate(
                    [piece] * ROPE_PACK, axis=1)
            return dst[r, :]

        cosf = per_token(cos_ref, rc_s)
        sin4 = per_token(sin_ref, rs_s)
        sinf = jnp.where(first_half, -sin4, sin4)

        def rope(tq):
            rot = jnp.where(first_half, pltpu.roll(tq, LANES - HEAD_DIM // 2, 1), pltpu.roll(tq, HEAD_DIM // 2, 1))
            return tq * cosf + rot * sinf

        for p in range(ATTN_WIDTH // LANES):
            q_s[r, p * LANES:(p + 1) * LANES] = (rope(proj[:, p * LANES:(p + 1) * LANES]) * scale).astype(BF16)

        def lane_variants(a, dst):
            ar = pltpu.roll(a, HEAD_DIM, 1)
            zero = jnp.zeros_like(a)
            rows = slice(WINDOW + r.start, WINDOW + r.stop)
            dst[0, rows, :] = jnp.where(low_head, a, zero).astype(BF16)
            dst[1, rows, :] = jnp.where(low_head, zero, ar).astype(BF16)
            dst[2, rows, :] = jnp.where(low_head, ar, zero).astype(BF16)
            dst[3, rows, :] = jnp.where(low_head, zero, a).astype(BF16)

        lane_variants(rope(proj[:, Q_END:K_END]), k_s)
        lane_variants(proj[:, K_END:V_END], v_s)

    projs = [qkv_proj(r) for r in halves]
    for r, proj in zip(halves, projs):
        rope_and_stage(r, proj)

    qi = lax.broadcasted_iota(jnp.int32, (WINDOW, 2 * WINDOW), 0)
    kj = lax.broadcasted_iota(jnp.int32, (WINDOW, 2 * WINDOW), 1)
    band = (kj > qi) & (kj <= qi + WINDOW)
    band_first = band & (kj >= jnp.where(t > 0, 0, WINDOW))

    def scores(b, g):
        rows = slice(b * WINDOW, (b + 1) * WINDOW)
        ctx = slice(b * WINDOW, (b + 2) * WINDOW)
        qg = jnp.concatenate([q_s[rows, (2 * g) * LANES:(2 * g + 1) * LANES],
                              q_s[rows, (2 * g + 1) * LANES:(2 * g + 2) * LANES]], axis=0)
        return [_dot_nt(qg, k_s[2 * g + j, ctx, :]) for j in range(2)]

    steps = [(b, g) for b in range(nblk) for g in range(N_KV_HEADS)]
    s_next = scores(*steps[0])
    for idx, (b, g) in enumerate(steps):
        mask = band_first if b == 0 else band
        rows = slice(b * WINDOW, (b + 1) * WINDOW)
        ctx = slice(b * WINDOW, (b + 2) * WINDOW)
        s_cur = s_next
        if idx < 3 * CONV_WIDTH // MXU_DIM:
            ccols = slice(idx * MXU_DIM, (idx + 1) * MXU_DIM)
            c_s[:, ccols] = _dot(h_s[...], win_ref[:, V_END + idx * MXU_DIM:V_END + (idx + 1) * MXU_DIM])
        if idx + 1 < len(steps):
            s_next = scores(*steps[idx + 1])
        probs = []
        for j in range(2):
            head_probs = []
            for r in range(2):
                sink = sinks_ref[4 * g + 2 * r + j]
                sr = jnp.where(mask, s_cur[j][r * WINDOW:(r + 1) * WINDOW], NEG_INF)
                m = jnp.maximum(jnp.max(sr, axis=-1, keepdims=True), sink)
                e = jnp.exp(sr - m)
                denom = jnp.sum(e, axis=-1, keepdims=True) + jnp.exp(sink - m)
                head_probs.append((e * (1.0 / denom)).astype(BF16))
            probs.append(jnp.concatenate(head_probs, axis=0))
        o = _dot(probs[0], v_s[2 * g, ctx, :]) + _dot(probs[1], v_s[2 * g + 1, ctx, :])
        y_s[rows, (2 * g) * LANES:(2 * g + 1) * LANES] = o[0:WINDOW]
        y_s[rows, (2 * g + 1) * LANES:(2 * g + 2) * LANES] = o[WINDOW:2 * WINDOW]

    for cchunk in range(nblk * N_KV_HEADS, 3 * CONV_WIDTH // MXU_DIM):
        c_s[:, cchunk * MXU_DIM:(cchunk + 1) * MXU_DIM] = _dot(
            h_s[...], win_ref[:, V_END + cchunk * MXU_DIM:V_END + (cchunk + 1) * MXU_DIM])

    u_s[SUBLANES:SUBLANES + ts, :] = c_s[:, CONV_WIDTH:2 * CONV_WIDTH] * c_s[:, 2 * CONV_WIDTH:3 * CONV_WIDTH]
    cw = convw_ref[...]

    def mix_out(r):
        conv = cw[0:1, :] * u_s[SUBLANES - 2 + r.start:SUBLANES - 2 + r.stop, :]
        conv = conv + cw[1:2, :] * u_s[SUBLANES - 1 + r.start:SUBLANES - 1 + r.stop, :]
        conv = conv + cw[2:3, :] * u_s[SUBLANES + r.start:SUBLANES + r.stop, :]
        y = jnp.concatenate([y_s[r, :], c_s[r, 0:CONV_WIDTH] * conv], axis=-1)
        sq = (y * y).astype(BF16)
        ms = jnp.concatenate([_dot(sq[:, c * MXU_DIM:(c + 1) * MXU_DIM], gsum_ref[...])
                              for c in range(D_MODEL // MXU_DIM)], axis=-1)
        yn = (y * lax.rsqrt(ms + NORM_EPS) * hnw_ref[...]).astype(BF16)
        return _dot(yn, wout_ref[...])

    def residual(r, out):
        xn = (xo_ref[r, :] if moe_in else x_ref[r, :]) + out
        xo_ref[r, :] = xn
        h2 = _rms(xn, fnw_ref[...])
        h2_ref[r, :] = h2.astype(h2_ref.dtype)
        return _dot(h2.astype(BF16), rw_ref[...]) if router else None

    def top2(r, lg):
        lanef = lane.astype(F32)
        l1 = jnp.where(lane < N_EXPERTS, lg, -jnp.inf)
        m1 = jnp.max(l1, axis=-1, keepdims=True)
        i1 = jnp.min(jnp.where(l1 == m1, lanef, float(LANES)), axis=-1, keepdims=True)
        l2 = jnp.where(lanef == i1, -jnp.inf, l1)
        m2 = jnp.max(l2, axis=-1, keepdims=True)
        i2 = jnp.min(jnp.where(l2 == m2, lanef, float(LANES)), axis=-1, keepdims=True)
        e2 = jnp.exp(m2 - m1)
        inv = 1.0 / (1.0 + e2)
        out = jnp.where(lane == 0, i1, jnp.where(lane == 1, i2, jnp.where(lane == 2, inv, e2 * inv)))
        rt_ref[r, :] = jnp.where(lane < 4, out, 0.0)

    outs = [mix_out(r) for r in halves]
    logits = [residual(r, out) for r, out in zip(halves, outs)]
    if router:
        for r, lg in zip(halves, logits):
            top2(r, lg)


def _resident(shape, layer=None):
    if layer is None:
        return pl.BlockSpec(shape, lambda *_: (0,) * len(shape), pipeline_mode=pl.Buffered(1))
    return pl.BlockSpec((None,) + shape, lambda *_: (layer,) + (0,) * len(shape), pipeline_mode=pl.Buffered(1))


def _mixer(x, moe, cosf, sinf, anw, win, sinks, convw, hnw, wout, fnw, gsum, rw, layer, seq, h2_dtype):
    n = x.shape[0]
    ts = min(SEQ_TILE, seq)
    nt = seq // ts
    nb = n // seq
    moe_in = moe is not None
    router = rw is not None
    row = lambda b, t: (b * nt + t, 0)
    tile = pl.BlockSpec((ts, D_MODEL), row)
    lanes_tile = pl.BlockSpec((ts, LANES), row)
    in_specs = [tile]
    args = [x]
    if moe_in:
        ymoe, rt_prev = moe
        off = n // ts
        in_specs += [tile, pl.BlockSpec((ts, D_MODEL), lambda b, t: (off + b * nt + t, 0)), lanes_tile]
        args += [ymoe, ymoe, rt_prev]
    rope_tile = pl.BlockSpec((ts // ROPE_PACK, LANES), row)
    in_specs += [rope_tile, rope_tile,
                 _resident((1, D_MODEL), layer), _resident((D_MODEL, IN_WIDTH), layer),
                 pl.BlockSpec(memory_space=pltpu.SMEM),
                 _resident((CONV_K, CONV_WIDTH), layer), _resident((1, D_MODEL), layer),
                 _resident((D_MODEL, D_MODEL), layer), _resident((1, D_MODEL), layer),
                 _resident((MXU_DIM, MXU_DIM))]
    args += [cosf, sinf, anw, win, sinks, convw, hnw, wout, fnw, gsum]
    out_shape = [jax.ShapeDtypeStruct((n, D_MODEL), F32), jax.ShapeDtypeStruct((n, D_MODEL), h2_dtype)]
    out_specs = [tile, tile]
    if router:
        in_specs.append(_resident((D_MODEL, LANES)))
        args.append(rw)
        out_shape.append(jax.ShapeDtypeStruct((n, LANES), F32))
        out_specs.append(lanes_tile)
    return pl.pallas_call(
        functools.partial(_mixer_kernel, ts, moe_in, router),
        grid=(nb, nt),
        in_specs=in_specs,
        out_specs=out_specs,
        out_shape=out_shape,
        scratch_shapes=[pltpu.VMEM((ts, ATTN_WIDTH), BF16),
                        pltpu.VMEM((4, WINDOW + ts, LANES), BF16),
                        pltpu.VMEM((4, WINDOW + ts, LANES), BF16),
                        pltpu.VMEM((SUBLANES + ts, CONV_WIDTH), F32),
                        pltpu.VMEM((ts, ATTN_WIDTH), F32),
                        pltpu.VMEM((ts, 3 * CONV_WIDTH), F32),
                        pltpu.VMEM((ts, D_MODEL), BF16),
                        pltpu.VMEM((ts, LANES), F32),
                        pltpu.VMEM((ts, LANES), F32)],
        compiler_params=pltpu.CompilerParams(dimension_semantics=("arbitrary", "arbitrary"),
                                             vmem_limit_bytes=VMEM_LIMIT_BYTES),
        name="mixer",
    )(*args)


def _swiglu(hb, wg_ref, wu_ref, wd_ref, acc_ref, side_work=None):
    for c in range(D_FF // FF_CHUNK):
        if side_work is not None:
            side_work(c)
        cols = slice(c * FF_CHUNK, (c + 1) * FF_CHUNK)
        g = _dot(hb, wg_ref[:, cols])
        u = _dot(hb, wu_ref[:, cols])
        a = (jax.nn.silu(g) * u).astype(BF16)
        part = _dot(a, wd_ref[cols, :])
        if c == 0:
            acc_ref[...] = part
        elif c < D_FF // FF_CHUNK - 1:
            acc_ref[...] += part
    return acc_ref[...] + part


def _ffn_kernel(n_cast, x_ref, h_ref, wg_ref, wu_ref, wd_ref, *refs):
    cast_in, o_ref, cast_out, acc_ref = refs[:n_cast], refs[n_cast], refs[n_cast + 1:-1], refs[-1]
    o_ref[...] = x_ref[...] + _swiglu(h_ref[...], wg_ref, wu_ref, wd_ref, acc_ref)
    for src, dst in zip(cast_in, cast_out):
        dst[...] = src[...].astype(dst.dtype)


def _dense_ffn(x, h2, wg, wu, wd, layer, cast=None):
    n = x.shape[0]
    tm = min(FFN_TILE, n)
    steps = n // tm
    tile = pl.BlockSpec((tm, D_MODEL), lambda i: (i, 0))
    in_specs = [tile, tile, _resident((D_MODEL, D_FF), layer), _resident((D_MODEL, D_FF), layer),
                _resident((D_FF, D_MODEL), layer)]
    out_specs = [tile]
    out_shape = [jax.ShapeDtypeStruct((n, D_MODEL), F32)]
    args = [x, h2, wg, wu, wd]
    if cast is not None:
        weights, cast_layer = cast
        per_expert = steps // N_EXPERTS
        assert per_expert * N_EXPERTS == steps
        for w in weights:
            rows, cols = w.shape[2] // per_expert, w.shape[3]
            in_specs.append(pl.BlockSpec((None, None, rows, cols),
                                         lambda i: (cast_layer, i // per_expert, i % per_expert, 0)))
            out_specs.append(pl.BlockSpec((None, rows, cols), lambda i: (i // per_expert, i % per_expert, 0)))
            out_shape.append(jax.ShapeDtypeStruct(w.shape[1:], BF16))
            args.append(w)
    outs = pl.pallas_call(
        functools.partial(_ffn_kernel, len(args) - 5),
        grid=(steps,),
        in_specs=in_specs,
        out_specs=out_specs,
        out_shape=out_shape,
        scratch_shapes=[pltpu.VMEM((tm, D_MODEL), F32)],
        compiler_params=pltpu.CompilerParams(dimension_semantics=("arbitrary",),
                                             vmem_limit_bytes=VMEM_LIMIT_BYTES),
        name="dense_ffn",
    )(*args)
    return outs[0], outs[1:]


def _moe_kernel(blk_e_ref, src0_ref, srcn_ref, dstp_ref, dstl_ref, h_hbm, wg_ref, wu_ref, wd_ref, y_hbm,
                xbuf, ybuf, acc_ref, gsem, ssem):
    i = pl.program_id(0)
    n_real = blk_e_ref[pl.num_programs(0)]
    slot = i % 2
    other = 1 - slot
    bm = MOE_BLOCK
    per_chunk = bm // MOE_ISSUE_CHUNKS

    def gather_row(idx_ref, r, block_slot):
        pltpu.make_async_copy(h_hbm.at[pl.ds(idx_ref[0, 0, r], 1)], xbuf.at[block_slot, pl.ds(r, 1)],
                              gsem.at[block_slot]).start(priority=0)

    def scatter_row(idx_ref, r, block_slot):
        pltpu.make_async_copy(ybuf.at[block_slot, pl.ds(r, 1)], y_hbm.at[pl.ds(idx_ref[0, 0, r], 1)],
                              ssem.at[block_slot]).start(priority=1)

    def wait_gather(block_slot):
        pltpu.make_async_copy(h_hbm.at[pl.ds(0, bm)], xbuf.at[block_slot], gsem.at[block_slot]).wait()

    def wait_scatter(block_slot):
        pltpu.make_async_copy(ybuf.at[block_slot], y_hbm.at[pl.ds(0, bm)], ssem.at[block_slot]).wait()

    @pl.when(i == 0)
    def _():
        def body(r, c):
            gather_row(src0_ref, r, 0)
            return c
        lax.fori_loop(0, bm, body, 0)
        ybuf[...] = jnp.zeros((2, bm, D_MODEL), F32)
        pltpu.make_async_copy(ybuf.at[0], y_hbm.at[pl.ds(y_hbm.shape[0] - bm, bm)], ssem.at[0]).start()

    @pl.when(i < n_real)
    def _():
        wait_gather(slot)

        def side_work(c):
            if c == 0:
                for r in range(bm):
                    gather_row(srcn_ref, r, other)
            nchunks = D_FF // FF_CHUNK
            first = 1 if nchunks > 1 else 0
            per = bm // (nchunks - first)
            if c >= first:
                for r in range((c - first) * per, (c - first + 1) * per):
                    scatter_row(dstp_ref, r, other)

        total = _swiglu(xbuf[slot].astype(BF16), wg_ref, wu_ref, wd_ref, acc_ref, side_work)
        wait_scatter(slot)
        ybuf[slot] = total

    @pl.when(i >= n_real)
    def _():
        ybuf[0] = jnp.zeros((bm, D_MODEL), F32)
        fill = pltpu.make_async_copy(ybuf.at[0], y_hbm.at[pl.ds(pl.multiple_of(i * bm, bm), bm)], ssem.at[0])
        fill.start()
        fill.wait()

    @pl.when(i == n_real - 1)
    def _():
        def body(r, c):
            scatter_row(dstl_ref, r, slot)
            return c
        lax.fori_loop(0, bm, body, 0)
        wait_scatter(slot)
        wait_scatter(other)
        wait_gather(other)


def _moe_experts(h2, blk_e, src, dst, wg, wu, wd):
    nblocks = src.shape[0]
    bm = MOE_BLOCK
    n_rows = (nblocks + 2) * bm
    wspec_in = pl.BlockSpec((None, D_MODEL, D_FF), lambda i, be: (be[i], 0, 0), pipeline_mode=pl.Buffered(1))
    wspec_out = pl.BlockSpec((None, D_FF, D_MODEL), lambda i, be: (be[i], 0, 0), pipeline_mode=pl.Buffered(1))
    smem_rows = lambda index_map: pl.BlockSpec((1, 1, bm), index_map, memory_space=pltpu.SMEM)
    grid_spec = pltpu.PrefetchScalarGridSpec(
        num_scalar_prefetch=1,
        grid=(nblocks,),
        in_specs=[smem_rows(lambda i, be: (0, 0, 0)),
                  smem_rows(lambda i, be: (jnp.minimum(i + 1, nblocks - 1), 0, 0)),
                  smem_rows(lambda i, be: (i, 0, 0)),
                  smem_rows(lambda i, be: (be[nblocks], 0, 0)),
                  pl.BlockSpec(memory_space=pl.ANY),
                  wspec_in, wspec_in, wspec_out],
        out_specs=pl.BlockSpec(memory_space=pl.ANY),
        scratch_shapes=[pltpu.VMEM((2, bm, D_MODEL), F32),
                        pltpu.VMEM((2, bm, D_MODEL), F32),
                        pltpu.VMEM((bm, D_MODEL), F32),
                        pltpu.SemaphoreType.DMA((2,)),
                        pltpu.SemaphoreType.DMA((2,))],
    )
    return pl.pallas_call(
        _moe_kernel,
        grid_spec=grid_spec,
        out_shape=jax.ShapeDtypeStruct((n_rows, D_MODEL), F32),
        compiler_params=pltpu.CompilerParams(dimension_semantics=("arbitrary",),
                                             vmem_limit_bytes=VMEM_LIMIT_BYTES),
        name="moe_experts",
    )(blk_e, src, src, dst, dst, h2, wg, wu, wd)


def _route(rt, n):
    bm = MOE_BLOCK
    n_assign = n * TOP_K
    n_rows = n_assign + N_EXPERTS * bm
    nblocks = n_rows // bm
    flat_e = rt[:, 0:TOP_K].astype(jnp.int32).reshape(-1)
    order = jnp.argsort(flat_e).astype(jnp.int32)
    experts = jnp.arange(N_EXPERTS, dtype=jnp.int32)
    sizes = jnp.sum((flat_e[:, None] == experts[None, :]).astype(jnp.int32), axis=0)
    padded = (sizes + bm - 1) // bm * bm
    group_end = jnp.cumsum(sizes)
    padded_end = jnp.cumsum(padded)
    blk_start = jnp.arange(nblocks, dtype=jnp.int32) * bm
    blk_e = jnp.minimum(jnp.sum((blk_start[:, None] >= padded_end[None, :]).astype(jnp.int32), axis=1),
                        N_EXPERTS - 1)
    rank = blk_start[:, None] + jnp.arange(bm, dtype=jnp.int32)[None, :] - (padded_end - padded)[blk_e][:, None]
    valid = rank < sizes[blk_e][:, None]
    sorted_pos = (group_end - sizes)[blk_e][:, None] + rank
    assign = order[jnp.clip(sorted_pos, 0, n_assign - 1)]
    tok = assign // TOP_K
    src = jnp.where(valid, tok, 0)
    slot = blk_start[:, None] + jnp.arange(bm, dtype=jnp.int32)[None, :]
    dst = jnp.where(valid, (assign % TOP_K) * n + tok, n_assign + slot - group_end[blk_e][:, None])
    spare = n_rows + jnp.arange(bm, dtype=jnp.int32)
    dst = jnp.concatenate([spare[None, :], dst], axis=0)
    n_real = padded_end[-1:] // bm
    return jnp.concatenate([blk_e, n_real]), src.reshape(nblocks, 1, bm), dst.reshape(nblocks + 1, 1, bm)


def _final_kernel(moe_in, *refs):
    if moe_in:
        x_ref, ya_ref, yb_ref, rt_ref, w_ref, o_ref = refs
        x = _moe_combine(x_ref[...], ya_ref, yb_ref, rt_ref)
    else:
        x_ref, w_ref, o_ref = refs
        x = x_ref[...]
    o_ref[...] = _rms(x, w_ref[...])


def _final(x, moe, w):
    n = x.shape[0]
    tm = min(1024, n)
    off = n // tm
    tile = pl.BlockSpec((tm, D_MODEL), lambda i: (i, 0))
    in_specs = [tile]
    args = [x]
    if moe is not None:
        in_specs += [tile, pl.BlockSpec((tm, D_MODEL), lambda i: (off + i, 0)),
                     pl.BlockSpec((tm, LANES), lambda i: (i, 0))]
        args += [moe[0], moe[0], moe[1]]
    return pl.pallas_call(
        functools.partial(_final_kernel, moe is not None),
        grid=(n // tm,),
        in_specs=in_specs + [_resident((1, D_MODEL))],
        out_specs=tile,
        out_shape=jax.ShapeDtypeStruct((n, D_MODEL), F32),
        compiler_params=pltpu.CompilerParams(dimension_semantics=("arbitrary",)),
        name="final_norm",
    )(*args, w)


def kernel(x, positions, attn_norm_w, w_in, attn_sinks, conv_w, head_norm_w, w_out, ffn_norm_w, dense_w_gate,
           dense_w_up, dense_w_down, router_w, moe_w_gate, moe_w_up, moe_w_down, final_norm_w):
    bsz, seq, _ = x.shape
    n = bsz * seq
    depth = w_in.shape[0]

    inv_freq = ROPE_THETA ** (-jnp.arange(0, HEAD_DIM, 2, dtype=F32) / HEAD_DIM)
    ang = (jnp.repeat(positions.astype(F32).reshape(n // ROPE_PACK, ROPE_PACK), HEAD_DIM // 2, axis=1)
           * jnp.tile(inv_freq, ROPE_PACK))
    cosf, sinf = jnp.cos(ang), jnp.sin(ang)

    grp = jnp.arange(MXU_DIM) // HEAD_DIM
    gsum = jnp.where(grp[:, None] == grp[None, :], 1.0 / HEAD_DIM, 0.0).astype(BF16)

    win_b, wout_b = w_in.astype(BF16), w_out.astype(BF16)
    dense_b = [w.astype(BF16) for w in (dense_w_gate, dense_w_up, dense_w_down)]
    moe_f = (moe_w_gate, moe_w_up, moe_w_down)
    rw_b = jnp.pad(router_w, ((0, 0), (0, 0), (0, LANES - N_EXPERTS))).astype(BF16)
    anw, hnw, fnw = (w[:, None, :] for w in (attn_norm_w, head_norm_w, ffn_norm_w))

    cast_in_dense = (n // min(FFN_TILE, n)) % N_EXPERTS == 0

    xf = x.reshape(n, D_MODEL)
    moe = None
    moe_b = ()
    for l in range(depth):
        is_moe = l % 2 == 1
        i = l // 2
        outs = _mixer(xf, moe, cosf, sinf, anw, win_b, attn_sinks[l], conv_w, hnw, wout_b, fnw, gsum,
                      rw_b[i] if is_moe else None, l, seq, F32 if is_moe else BF16)
        if is_moe:
            xf, h2, rt = outs
            blk_e, src, dst = _route(rt, n)
            if not moe_b:
                moe_b = [w[i].astype(BF16) for w in moe_f]
            moe = (_moe_experts(h2, blk_e, src, dst, *moe_b), rt)
            moe_b = ()
        else:
            xf, h2 = outs
            cast = (moe_f, i) if cast_in_dense and l + 1 < depth else None
            xf, moe_b = _dense_ffn(xf, h2, *dense_b, i, cast)
            moe = None
    out = _final(xf, moe, final_norm_w[None])
    return out.reshape(bsz, seq, D_MODEL)
```
